```python
import jax, jax.numpy as jnp
from jax import lax
import numpy as np

D_MODEL = 2048
BATCH = 8
SEQ = 4096
DEPTH = 4

GRID_W = 64
CTX_LEN = 256
EXPAND = 2
D_INNER = EXPAND * D_MODEL
A_WIDTH = D_INNER // 2
H_A = 8
DV_A = A_WIDTH // H_A
DK_A = DV_A // 2
QK_WIDTH = H_A * DK_A
B_WIDTH = D_INNER - A_WIDTH
G_B = 8
GB_DIM = B_WIDTH // G_B
EVEN_IN = 2 * QK_WIDTH + 2 * A_WIDTH + 2 * B_WIDTH
C_WIDTH = D_INNER
ODD_IN = 4 * C_WIDTH
CONV_W = 3
CHUNK = 128
ROPE_BASE = 10000.0
EPS = 1e-6
N_EVEN = (DEPTH + 1) // 2
N_ODD = DEPTH // 2

kernel_name = "hybrid_retnet_fnet_shortconv_prefix_dit"


def _rmsnorm(x, w):
    xf = x.astype(jnp.float32)
    y = xf * lax.rsqrt(jnp.mean(xf * xf, axis=-1, keepdims=True) + EPS)
    return (y * w.astype(jnp.float32)).astype(x.dtype)


def _modulate(x, w, shift, scale):
    return _rmsnorm(x, w) * (1.0 + scale) + shift


def _heads(t, d):
    b, L, _ = t.shape
    return t.reshape(b, L, -1, d).transpose(0, 2, 1, 3).astype(jnp.float32)


def _axial_rope(t):
    L = t.shape[2]
    pos = jnp.arange(L)
    row = (pos // GRID_W).astype(jnp.float32)
    col = (pos % GRID_W).astype(jnp.float32)
    nf = DK_A // 4
    inv = ROPE_BASE ** (-jnp.arange(nf, dtype=jnp.float32) / nf)
    ang = jnp.concatenate([row[:, None] * inv[None], col[:, None] * inv[None]], axis=-1)
    cos, sin = jnp.cos(ang), jnp.sin(ang)
    t1, t2 = t[..., : DK_A // 2], t[..., DK_A // 2:]
    return jnp.concatenate([t1 * cos - t2 * sin, t1 * sin + t2 * cos], axis=-1)


def _retention_scan(q, k, v, log_gamma, s0, inclusive):
    b, h, L, dk = q.shape
    dv = v.shape[-1]
    nc = L // CHUNK
    qc = q.reshape(b, h, nc, CHUNK, dk)
    kc = k.reshape(b, h, nc, CHUNK, dk)
    vc = v.reshape(b, h, nc, CHUNK, dv)
    idx = jnp.arange(CHUNK, dtype=jnp.float32)
    rel = idx[:, None] - idx[None, :]
    keep = (rel >= 0) if inclusive else (rel > 0)
    dmat = jnp.where(keep[None], jnp.exp(log_gamma[:, None, None] * jnp.maximum(rel, 0.0)[None]), 0.0)
    scores = jnp.einsum('bhnid,bhnjd->bhnij', qc, kc) * dmat[None, :, None]
    o = jnp.einsum('bhnij,bhnje->bhnie', scores, vc)
    zeta = jnp.exp(log_gamma[:, None] * (CHUNK - 1.0 - idx)[None])
    inc = jnp.einsum('bhnjd,bhnje->bhnde', kc, vc * zeta[None, :, None, :, None])
    chunk_decay = jnp.exp(log_gamma * CHUNK)[None, :, None, None]

    def step(s, inc_n):
        return chunk_decay * s + inc_n, s

    s_last, s_prev = lax.scan(step, s0, jnp.moveaxis(inc, 2, 0))
    s_prev = jnp.moveaxis(s_prev, 0, 2)
    xi = jnp.exp(log_gamma[:, None] * (idx + 1.0)[None])
    o = o + jnp.einsum('bhnid,bhnde->bhnie', qc, s_prev) * xi[None, :, None, :, None]
    return o.reshape(b, h, L, dv), s_last


def _retention(q, k, v, log_gamma, s0_fwd, s0_bwd):
    flip = lambda t: jnp.flip(t, axis=2)
    o_f, s_f = _retention_scan(q, k, v, log_gamma[0], s0_fwd, True)
    o_b, s_b = _retention_scan(flip(q), flip(k), flip(v), log_gamma[1], s0_bwd, False)
    o = o_f + flip(o_b)
    o = o * lax.rsqrt(jnp.mean(o * o, axis=-1, keepdims=True) + EPS)
    return o, s_f, s_b


def _merge_heads(o, dtype):
    b, h, L, d = o.shape
    return o.transpose(0, 2, 1, 3).reshape(b, L, h * d).astype(dtype)


def _fourier(u, fno_w):
    b, L, _ = u.shape
    ug = u.reshape(b, L, G_B, GB_DIM).astype(jnp.float32)
    f = jnp.fft.fft2(ug, axes=(1, 3), norm='ortho').real
    y = jnp.einsum('blgc,gcd->blgd', f.astype(u.dtype), fno_w)
    return y.reshape(b, L, B_WIDTH)


def _split_even(p):
    cuts = [QK_WIDTH, 2 * QK_WIDTH, 2 * QK_WIDTH + A_WIDTH, 2 * QK_WIDTH + 2 * A_WIDTH,
            2 * QK_WIDTH + 2 * A_WIDTH + B_WIDTH]
    return jnp.split(p, cuts, axis=-1)


def _even_layer(hx, hc, w_in, decay_logit, fno_w, w_out, need_ctx):
    log_gamma = -jnp.exp(decay_logit.astype(jnp.float32))
    qx, kx, vx, gax, ux, gbx = _split_even(hx @ w_in)
    qc, kc, vc, gac, uc, gbc = _split_even(hc @ w_in)
    kscale = DK_A ** -0.5
    b = hx.shape[0]
    zeros = jnp.zeros((b, H_A, DK_A, DV_A), jnp.float32)
    ret_c, s_f, s_b = _retention(_heads(qc, DK_A), _heads(kc, DK_A) * kscale, _heads(vc, DV_A),
                                 log_gamma, zeros, zeros)
    ret_x, _, _ = _retention(_axial_rope(_heads(qx, DK_A)), _axial_rope(_heads(kx, DK_A)) * kscale,
                             _heads(vx, DV_A), log_gamma, s_f, s_b)
    yx = jnp.concatenate([_merge_heads(ret_x, hx.dtype) * jax.nn.silu(gax),
                          _fourier(ux, fno_w) * jax.nn.silu(gbx)], axis=-1) @ w_out
    yc = None
    if need_ctx:
        yc = jnp.concatenate([_merge_heads(ret_c, hc.dtype) * jax.nn.silu(gac),
                              _fourier(uc, fno_w) * jax.nn.silu(gbc)], axis=-1) @ w_out
    return yx, yc


def _conv3(z, w, rows):
    b, L, ch = z.shape
    if rows is None:
        zp = jnp.pad(z, ((0, 0), (1, 1), (0, 0)))
        return zp[:, :-2] * w[0] + zp[:, 1:-1] * w[1] + zp[:, 2:] * w[2]
    zg = z.reshape(b, rows, GRID_W, ch)
    zp = jnp.pad(zg, ((0, 0), (0, 0), (1, 1), (0, 0)))
    y = zp[:, :, :-2] * w[0] + zp[:, :, 1:-1] * w[1] + zp[:, :, 2:] * w[2]
    return y.reshape(b, L, ch)


def _odd_layer(h, w_in, conv_w, w_out, rows):
    bg, cg, xt, g = jnp.split(h @ w_in, 4, axis=-1)
    y = bg * _conv3(cg * xt, conv_w, rows) * jax.nn.silu(g)
    return y @ w_out


def setup_inputs(seed: int = 0) -> dict:
    key = jax.random.key(seed)
    ks = jax.random.split(key, 15)

    def nrm(k, shape, scale):
        return jax.random.normal(k, shape, jnp.float32) * scale

    base = jnp.asarray(np.log(-np.log1p(-2.0 ** (-5.0 - np.arange(H_A)))), jnp.float32)
    return {
        'x': nrm(ks[0], (BATCH, SEQ, D_MODEL), 1.0),
        'c': nrm(ks[1], (BATCH, D_MODEL), 1.0),
        'ctx': nrm(ks[2], (BATCH, CTX_LEN, D_MODEL), 1.0),
        'c_ctx': nrm(ks[3], (D_MODEL,), 1.0),
        'ada_w': nrm(ks[4], (DEPTH, D_MODEL, 3 * D_MODEL), 0.5 * D_MODEL ** -0.5),
        'ada_b': nrm(ks[5], (DEPTH, 3 * D_MODEL), 0.02),
        'norm_w': 1.0 + nrm(ks[6], (DEPTH, D_MODEL), 0.02),
        'ev_w_in': nrm(ks[7], (N_EVEN, D_MODEL, EVEN_IN), D_MODEL ** -0.5),
        'ret_decay_logit': base[None, None, :] + nrm(ks[8], (N_EVEN, 2, H_A), 0.1),
        'fno_w': nrm(ks[9], (N_EVEN, G_B, GB_DIM, GB_DIM), GB_DIM ** -0.5),
        'ev_w_out': nrm(ks[10], (N_EVEN, D_INNER, D_MODEL), D_INNER ** -0.5),
        'od_w_in': nrm(ks[11], (N_ODD, D_MODEL, ODD_IN), D_MODEL ** -0.5),
        'conv_w': nrm(ks[12], (N_ODD, CONV_W, C_WIDTH), CONV_W ** -0.5),
        'od_w_out': nrm(ks[13], (N_ODD, C_WIDTH, D_MODEL), C_WIDTH ** -0.5),
        'final_norm_w': 1.0 + nrm(ks[14], (D_MODEL,), 0.02),
    }


def reference(x, c, ctx, c_ctx, ada_w, ada_b, norm_w, ev_w_in, ret_decay_logit, fno_w, ev_w_out,
              od_w_in, conv_w, od_w_out, final_norm_w):
    rows = x.shape[1] // GRID_W
    sc = jax.nn.silu(c)
    scc = jax.nn.silu(c_ctx)
    for i in range(DEPTH):
        need_ctx = i < DEPTH - 1
        shift_x, scale_x, gate_x = jnp.split((sc @ ada_w[i] + ada_b[i])[:, None, :], 3, axis=-1)
        shift_c, scale_c, gate_c = jnp.split(scc @ ada_w[i] + ada_b[i], 3, axis=-1)
        hx = _modulate(x, norm_w[i], shift_x, scale_x)
        hc = _modulate(ctx, norm_w[i], shift_c, scale_c)
        j = i // 2
        if i % 2 == 0:
            yx, yc = _even_layer(hx, hc, ev_w_in[j], ret_decay_logit[j], fno_w[j], ev_w_out[j], need_ctx)
        else:
            yx = _odd_layer(hx, od_w_in[j], conv_w[j], od_w_out[j], rows)
            yc = _odd_layer(hc, od_w_in[j], conv_w[j], od_w_out[j], None) if need_ctx else None
        x = x + gate_x * yx
        if need_ctx:
            ctx = ctx + gate_c * yc
    return _rmsnorm(x, final_norm_w)
```

```python
import functools

import numpy as np
import jax
import jax.numpy as jnp
from jax import lax
from jax.experimental import pallas as pl
from jax.experimental.pallas import tpu as pltpu

F32 = jnp.float32
BF16 = jnp.bfloat16

D_MODEL = 2048
DEPTH = 4
GRID_W = 64
H_A = 8
DK_A = 128
DV_A = 256
QK_WIDTH = H_A * DK_A
A_WIDTH = H_A * DV_A
G_B = 8
GB_DIM = 256
B_WIDTH = G_B * GB_DIM
EVEN_IN = 2 * QK_WIDTH + 2 * A_WIDTH + 2 * B_WIDTH
C_WIDTH = 4096
ROPE_BASE = 10000.0
EPS = 1e-6
RET_CHUNK = 256
DFT_RADIX = 64

P_WIDTH = 2 * QK_WIDTH + 2 * A_WIDTH + B_WIDTH
P_V0 = 2 * QK_WIDTH
P_GA0 = P_V0 + A_WIDTH
P_GB0 = P_GA0 + A_WIDTH

VMEM_LIMIT = 56 * 1024 * 1024


def _cparams(sem):
    return pltpu.CompilerParams(dimension_semantics=sem, vmem_limit_bytes=VMEM_LIMIT)


def _silu(x):
    return x * (1.0 / (1.0 + jnp.exp(-x)))


def _norm_mod(x, nw, sh, sc):
    y = x * lax.rsqrt(jnp.mean(x * x, axis=-1, keepdims=True) + EPS)
    return (y * nw) * (1.0 + sc) + sh


ADA_TN = 768


def _ada_kernel(s_ref, w_ref, b_ref, o_ref):
    s = _silu(s_ref[...]).astype(BF16)
    w = w_ref[...].astype(BF16)
    o_ref[...] = jnp.dot(s, w, preferred_element_type=F32) + b_ref[...]


def _ada_mod(s, ada_w, ada_b):
    depth, d, n3 = ada_w.shape
    rows = s.shape[0]
    return pl.pallas_call(
        _ada_kernel,
        grid=(depth, n3 // ADA_TN),
        in_specs=[
            pl.BlockSpec((rows, d), lambda l, n: (0, 0)),
            pl.BlockSpec((None, d, ADA_TN), lambda l, n: (l, 0, n)),
            pl.BlockSpec((None, 1, ADA_TN), lambda l, n: (l, 0, n)),
        ],
        out_specs=pl.BlockSpec((None, rows, ADA_TN), lambda l, n: (l, 0, n)),
        out_shape=jax.ShapeDtypeStruct((depth, rows, n3), F32),
        compiler_params=_cparams(("arbitrary", "arbitrary")),
        name="ada_mod",
    )(s, ada_w, ada_b.reshape(depth, 1, n3))


EV_TM = 1024
EV_TN = 1024
EV_NQK = 2 * QK_WIDTH // EV_TN
EV_NV1 = (P_GA0) // EV_TN
EV_NGA1 = (P_GB0) // EV_TN
EV_NU1 = EV_NGA1 + B_WIDTH // EV_TN
EV_NT = EVEN_IN // EV_TN


def _inproj_even_kernel(x_ref, sh_ref, sc_ref, nw_ref, w_ref, cos_ref, sin_ref,
                        p_ref, u_ref, h_ref, *, kscale):
    n = pl.program_id(2)

    @pl.when(n == 0)
    def _():
        h_ref[...] = _norm_mod(x_ref[...], nw_ref[...], sh_ref[...], sc_ref[...]).astype(BF16)

    acc = jnp.dot(h_ref[...], w_ref[...], preferred_element_type=F32)

    @pl.when(n < EV_NQK)
    def _():
        is_k = n >= (QK_WIDTH // EV_TN)
        scale = jnp.where(is_k, jnp.float32(kscale), jnp.float32(1.0))
        cosf = cos_ref[...]
        sinf = sin_ref[...]
        for hh in range(EV_TN // DK_A):
            t = acc[:, hh * DK_A:(hh + 1) * DK_A]
            rot = pltpu.roll(t, DK_A // 2, 1)
            p_ref[:, hh * DK_A:(hh + 1) * DK_A] = ((t * cosf + rot * sinf) * scale).astype(BF16)

    @pl.when(jnp.logical_and(n >= EV_NQK, n < EV_NV1))
    def _():
        p_ref[...] = acc.astype(BF16)

    @pl.when(jnp.logical_or(jnp.logical_and(n >= EV_NV1, n < EV_NGA1), n >= EV_NU1))
    def _():
        p_ref[...] = _silu(acc).astype(BF16)

    @pl.when(jnp.logical_and(n >= EV_NGA1, n < EV_NU1))
    def _():
        u_ref[...] = acc.astype(BF16)


def _inproj_even(x, shift, scale, nw, w, cosf, sinf):
    bx, lx, d = x.shape
    tm = min(EV_TM, lx)
    n_u = B_WIDTH // EV_TN

    def p_map(b, i, n):
        return (b, i, jnp.where(n < EV_NGA1, n, jnp.where(n < EV_NU1, EV_NGA1 - 1, n - n_u)))

    def u_map(b, i, n):
        return (b, i, jnp.clip(n - EV_NGA1, 0, n_u - 1))

    return pl.pallas_call(
        functools.partial(_inproj_even_kernel, kscale=DK_A ** -0.5),
        grid=(bx, lx // tm, EV_NT),
        in_specs=[
            pl.BlockSpec((None, tm, d), lambda b, i, n: (b, i, 0)),
            pl.BlockSpec((None, 1, d), lambda b, i, n: (b, 0, 0)),
            pl.BlockSpec((None, 1, d), lambda b, i, n: (b, 0, 0)),
            pl.BlockSpec((1, d), lambda b, i, n: (0, 0)),
            pl.BlockSpec((d, EV_TN), lambda b, i, n: (0, n)),
            pl.BlockSpec((tm, DK_A), lambda b, i, n: (i, 0)),
            pl.BlockSpec((tm, DK_A), lambda b, i, n: (i, 0)),
        ],
        out_specs=[
            pl.BlockSpec((None, tm, EV_TN), p_map),
            pl.BlockSpec((None, tm, EV_TN), u_map),
        ],
        out_shape=[
            jax.ShapeDtypeStruct((bx, lx, P_WIDTH), BF16),
            jax.ShapeDtypeStruct((bx, lx, B_WIDTH), BF16),
        ],
        scratch_shapes=[pltpu.VMEM((tm, d), BF16)],
        compiler_params=_cparams(("arbitrary", "arbitrary", "arbitrary")),
        name="inproj_even",
    )(x, shift, scale, nw, w, cosf, sinf)


def _ret_kernel(lg_ref, q_ref, k_ref, v_ref, g_ref, s0f_ref, s0b_ref,
                y_ref, sf_ref, sb_ref, o_scr, *, seq, chunk):
    h = pl.program_id(1)
    lgf = lg_ref[0, h]
    lgb = lg_ref[1, h]
    nc = seq // chunk
    cf = float(chunk)

    ii = lax.broadcasted_iota(jnp.int32, (chunk, chunk), 0)
    jj = lax.broadcasted_iota(jnp.int32, (chunk, chunk), 1)
    rel = (ii - jj).astype(F32)
    dmat = jnp.where(rel >= 0.0,
                     jnp.exp(lgf * jnp.maximum(rel, 0.0)),
                     jnp.exp(lgb * jnp.maximum(-rel, 0.0)))
    row = lax.broadcasted_iota(jnp.int32, (chunk, 1), 0).astype(F32)
    xi_f = jnp.exp(lgf * (row + 1.0))
    zeta_f = jnp.exp(lgf * (cf - 1.0 - row))
    xi_b = jnp.exp(lgb * (cf - row))
    zeta_b = jnp.exp(lgb * row)
    dec_f = jnp.exp(jnp.full((1, DV_A), lgf * cf, F32))
    dec_b = jnp.exp(jnp.full((1, DV_A), lgb * cf, F32))

    def state_update(s, kn, vn, zeta, dec):
        kz = (kn.astype(F32) * zeta).astype(BF16)
        inc = lax.dot_general(kz, vn, (((0,), (0,)), ((), ())), preferred_element_type=F32)
        return dec * s + inc

    def fwd(n, s):
        off = pl.multiple_of(n * chunk, chunk)
        qn = q_ref[pl.ds(off, chunk), :]
        kn = k_ref[pl.ds(off, chunk), :]
        vn = v_ref[pl.ds(off, chunk), :]
        sc = lax.dot_general(qn, kn, (((1,), (1,)), ((), ())), preferred_element_type=F32)
        o = jnp.dot((sc * dmat).astype(BF16), vn, preferred_element_type=F32)
        o = o + jnp.dot(qn, s.astype(BF16), preferred_element_type=F32) * xi_f
        o_scr[pl.ds(off, chunk), :] = o
        return state_update(s, kn, vn, zeta_f, dec_f)

    sf_ref[...] = lax.fori_loop(0, nc, fwd, s0f_ref[...])

    def bwd(t, s):
        off = pl.multiple_of((nc - 1 - t) * chunk, chunk)
        qn = q_ref[pl.ds(off, chunk), :]
        kn = k_ref[pl.ds(off, chunk), :]
        vn = v_ref[pl.ds(off, chunk), :]
        o = o_scr[pl.ds(off, chunk), :]
        o = o + jnp.dot(qn, s.astype(BF16), preferred_element_type=F32) * xi_b
        o = o * lax.rsqrt(jnp.mean(o * o, axis=-1, keepdims=True) + EPS)
        y_ref[pl.ds(off, chunk), :] = (o * g_ref[pl.ds(off, chunk), :].astype(F32)).astype(BF16)
        return state_update(s, kn, vn, zeta_b, dec_b)

    sb_ref[...] = lax.fori_loop(0, nc, bwd, s0b_ref[...])


def _ret(p, s0f, s0b, log_gamma):
    b, seq, _ = p.shape
    chunk = min(RET_CHUNK, seq)
    st_spec = pl.BlockSpec((None, None, DK_A, DV_A), lambda bb, hh: (bb, hh, 0, 0))
    st_shape = jax.ShapeDtypeStruct((b, H_A, DK_A, DV_A), F32)
    return pl.pallas_call(
        functools.partial(_ret_kernel, seq=seq, chunk=chunk),
        grid=(b, H_A),
        in_specs=[
            pl.BlockSpec(memory_space=pltpu.SMEM),
            pl.BlockSpec((None, seq, DK_A), lambda bb, hh: (bb, 0, hh)),
            pl.BlockSpec((None, seq, DK_A), lambda bb, hh: (bb, 0, QK_WIDTH // DK_A + hh)),
            pl.BlockSpec((None, seq, DV_A), lambda bb, hh: (bb, 0, P_V0 // DV_A + hh)),
            pl.BlockSpec((None, seq, DV_A), lambda bb, hh: (bb, 0, P_GA0 // DV_A + hh)),
            st_spec, st_spec,
        ],
        out_specs=[
            pl.BlockSpec((None, seq, DV_A), lambda bb, hh: (bb, 0, hh)),
            st_spec, st_spec,
        ],
        out_shape=[jax.ShapeDtypeStruct((b, seq, A_WIDTH), BF16), st_shape, st_shape],
        scratch_shapes=[pltpu.VMEM((seq, DV_A), F32)],
        compiler_params=_cparams(("arbitrary", "arbitrary")),
        name="ret",
    )(log_gamma, p, p, p, p, s0f, s0b)


def _dft_cos_sin(n):
    kk = (np.arange(n)[:, None] * np.arange(n)[None, :]) % n
    ang = 2.0 * np.pi * kk / n
    return np.cos(ang), np.sin(ang)


def _fold_kernel(cc_ref, sc_ref, w_ref, o_ref):
    w = w_ref[...]
    o_ref[:, :GB_DIM] = jnp.dot(cc_ref[...], w, precision=lax.Precision.HIGHEST,
                                preferred_element_type=F32).astype(BF16)
    o_ref[:, GB_DIM:] = (-jnp.dot(sc_ref[...], w, precision=lax.Precision.HIGHEST,
                                  preferred_element_type=F32)).astype(BF16)


def _fold_fno(fno_w):
    cc, sc = _dft_cos_sin(GB_DIM)
    mat = pl.BlockSpec((GB_DIM, GB_DIM), lambda g: (0, 0))
    return pl.pallas_call(
        _fold_kernel,
        grid=(G_B,),
        in_specs=[mat, mat, pl.BlockSpec((None, GB_DIM, GB_DIM), lambda g: (g, 0, 0))],
        out_specs=pl.BlockSpec((None, GB_DIM, 2 * GB_DIM), lambda g: (g, 0, 0)),
        out_shape=jax.ShapeDtypeStruct((G_B, GB_DIM, 2 * GB_DIM), BF16),
        compiler_params=_cparams(("arbitrary",)),
        name="fold_fno",
    )(jnp.asarray(cc, F32), jnp.asarray(sc, F32), fno_w)


CH_TM = 2048


def _chan_kernel(u_ref, m_ref, br_ref, bi_ref):
    r = jnp.dot(u_ref[...], m_ref[...], preferred_element_type=F32)
    br_ref[...] = r[:, :GB_DIM].astype(BF16)
    bi_ref[...] = r[:, GB_DIM:].astype(BF16)


def _chan_dft(u, m12):
    bx, lx, _ = u.shape
    tm = min(CH_TM, lx)
    col = pl.BlockSpec((None, tm, GB_DIM), lambda b, i, g: (b, i, g))
    shp = jax.ShapeDtypeStruct((bx, lx, B_WIDTH), BF16)
    return pl.pallas_call(
        _chan_kernel,
        grid=(bx, lx // tm, G_B),
        in_specs=[col, pl.BlockSpec((None, GB_DIM, 2 * GB_DIM), lambda b, i, g: (g, 0, 0))],
        out_specs=[col, col],
        out_shape=[shp, shp],
        compiler_params=_cparams(("arbitrary", "arbitrary", "arbitrary")),
        name="chan_dft",
    )(u, m12)


SWAP_T = 16
SLOW_T = 8


def _chan_swap_kernel(u_ref, m_ref, br_ref, bi_ref):
    r = DFT_RADIX
    x = jnp.swapaxes(u_ref[...], 0, 1).reshape(SWAP_T * r, B_WIDTH)
    for g in range(G_B):
        cols = slice(g * GB_DIM, (g + 1) * GB_DIM)
        y = jnp.dot(x[:, cols], m_ref[g], preferred_element_type=F32)
        br_ref[:, :, cols] = y[:, :GB_DIM].reshape(SWAP_T, r, GB_DIM).astype(BF16)
        bi_ref[:, :, cols] = y[:, GB_DIM:].reshape(SWAP_T, r, GB_DIM).astype(BF16)


def _chan_swap(u, m12):
    b = u.shape[0]
    r = DFT_RADIX
    shp = jax.ShapeDtypeStruct((b, r, r, B_WIDTH), BF16)
    out = pl.BlockSpec((None, SWAP_T, r, B_WIDTH), lambda bb, j: (bb, j, 0, 0))
    return pl.pallas_call(
        _chan_swap_kernel,
        grid=(b, r // SWAP_T),
        in_specs=[pl.BlockSpec((None, r, SWAP_T, B_WIDTH), lambda bb, j: (bb, 0, j, 0)),
                  pl.BlockSpec((G_B, GB_DIM, 2 * GB_DIM), lambda bb, j: (0, 0, 0))],
        out_specs=[out, out],
        out_shape=[shp, shp],
        compiler_params=_cparams(("arbitrary", "arbitrary")),
        name="chan_swap",
    )(u.reshape(b, r, r, B_WIDTH), m12)


def _slow_matrices():
    n = DFT_RADIX * DFT_RADIX
    bb = np.arange(DFT_RADIX)[:, None, None]
    k1 = np.arange(DFT_RADIX)[None, :, None]
    a = np.arange(DFT_RADIX)[None, None, :]
    ang = 2.0 * np.pi * ((DFT_RADIX * a * k1 + bb * k1) % n) / n
    c, s = np.cos(ang), np.sin(ang)
    return np.concatenate([np.concatenate([c, s], axis=2), np.concatenate([-s, c], axis=2)], axis=1)


def _fast_matrix():
    c, s = _dft_cos_sin(DFT_RADIX)
    return np.concatenate([c, s], axis=1)


def _slow_kernel(w_ref, xr_ref, xi_ref, y_ref):
    for bl in range(SLOW_T):
        x = jnp.concatenate([xr_ref[bl], xi_ref[bl]], axis=0)
        y = jnp.dot(w_ref[bl], x, preferred_element_type=F32)
        y_ref[0, bl] = y[:DFT_RADIX].astype(BF16)
        y_ref[1, bl] = y[DFT_RADIX:].astype(BF16)


def _dft_slow(br, bi):
    b = br.shape[0]
    r = DFT_RADIX
    xin = pl.BlockSpec((None, SLOW_T, r, B_WIDTH), lambda bb, j: (bb, j, 0, 0))
    return pl.pallas_call(
        _slow_kernel,
        grid=(b, r // SLOW_T),
        in_specs=[pl.BlockSpec((SLOW_T, 2 * r, 2 * r), lambda bb, j: (j, 0, 0)), xin, xin],
        out_specs=pl.BlockSpec((None, 2, SLOW_T, r, B_WIDTH), lambda bb, j: (bb, 0, j, 0, 0)),
        out_shape=jax.ShapeDtypeStruct((b, 2, r, r, B_WIDTH), BF16),
        compiler_params=_cparams(("arbitrary", "arbitrary")),
        name="dft_slow",
    )(jnp.asarray(_slow_matrices(), BF16), br, bi)


def _fast_kernel(w_ref, y_ref, g_ref, o_ref, o_scr, *, scale):
    zr = jnp.swapaxes(y_ref[0], 0, 1)
    zi = jnp.swapaxes(y_ref[1], 0, 1)
    w = w_ref[...]
    for kl in range(SWAP_T):
        z = jnp.concatenate([zr[kl], zi[kl]], axis=0)
        o_scr[kl] = jnp.dot(w, z, preferred_element_type=F32).astype(BF16)
    o = jnp.swapaxes(o_scr[...], 0, 1).astype(F32)
    o_ref[...] = ((o * scale) * g_ref[...].astype(F32)).astype(BF16)


def _dft_fast(y, p):
    b = y.shape[0]
    r = DFT_RADIX
    seq = r * r
    blk = pl.BlockSpec((None, r, SWAP_T, B_WIDTH), lambda bb, j: (bb, 0, j, 0))
    out = pl.pallas_call(
        functools.partial(_fast_kernel, scale=float((seq * GB_DIM) ** -0.5)),
        grid=(b, r // SWAP_T),
        in_specs=[
            pl.BlockSpec((r, 2 * r), lambda bb, j: (0, 0)),
            pl.BlockSpec((None, 2, r, SWAP_T, B_WIDTH), lambda bb, j: (bb, 0, 0, j, 0)),
            pl.BlockSpec((None, r, SWAP_T, B_WIDTH), lambda bb, j: (bb, 0, j, P_GB0 // B_WIDTH)),
        ],
        out_specs=blk,
        out_shape=jax.ShapeDtypeStruct((b, r, r, B_WIDTH), BF16),
        scratch_shapes=[pltpu.VMEM((SWAP_T, r, B_WIDTH), BF16)],
        compiler_params=_cparams(("arbitrary", "arbitrary")),
        name="dft_fast",
    )(jnp.asarray(_fast_matrix(), BF16), y, p.reshape(b, r, r, P_WIDTH))
    return out.reshape(b, seq, B_WIDTH)


def _ctx_dft_kernel(w_ref, xr_ref, xi_ref, g_ref, o_ref, *, scale):
    x = jnp.concatenate([xr_ref[...], xi_ref[...]], axis=0)
    o = jnp.dot(w_ref[...], x, preferred_element_type=F32)
    o_ref[...] = ((o * scale) * g_ref[...].astype(F32)).astype(BF16)


def _dft_ctx(br, bi, p):
    b, seq, _ = br.shape
    c, s = _dft_cos_sin(seq)
    w = jnp.asarray(np.concatenate([c, s], axis=1), BF16)
    blk = pl.BlockSpec((None, seq, B_WIDTH), lambda bb: (bb, 0, 0))
    return pl.pallas_call(
        functools.partial(_ctx_dft_kernel, scale=float((seq * GB_DIM) ** -0.5)),
        grid=(b,),
        in_specs=[pl.BlockSpec((seq, 2 * seq), lambda bb: (0, 0)), blk, blk,
                  pl.BlockSpec((None, seq, B_WIDTH), lambda bb: (bb, 0, P_GB0 // B_WIDTH))],
        out_specs=blk,
        out_shape=jax.ShapeDtypeStruct((b, seq, B_WIDTH), BF16),
        compiler_params=_cparams(("arbitrary",)),
        name="dft_ctx",
    )(w, br, bi, p)


OD_TM = 512
OD_TN = 512


def _inproj_odd_kernel(x_ref, sh_ref, sc_ref, nw_ref, wb_ref, wc_ref, wx_ref, wg_ref, cw_ref,
                       y_ref, h_ref, *, period):
    @pl.when(pl.program_id(2) == 0)
    def _():
        h_ref[...] = _norm_mod(x_ref[...], nw_ref[...], sh_ref[...], sc_ref[...]).astype(BF16)

    h = h_ref[...]
    tm = h.shape[0]
    z = (jnp.dot(h, wc_ref[...], preferred_element_type=F32)
         * jnp.dot(h, wx_ref[...], preferred_element_type=F32))
    pos = lax.broadcasted_iota(jnp.int32, (tm, 1), 0) % period
    z_prev = jnp.where(pos == 0, 0.0, pltpu.roll(z, 1, 0))
    z_next = jnp.where(pos == period - 1, 0.0, pltpu.roll(z, tm - 1, 0))
    cw = cw_ref[...]
    conv = z_prev * cw[0:1, :] + z * cw[1:2, :] + z_next * cw[2:3, :]
    bg = jnp.dot(h, wb_ref[...], preferred_element_type=F32)
    g = jnp.dot(h, wg_ref[...], preferred_element_type=F32)
    y_ref[...] = (bg * conv * _silu(g)).astype(BF16)


def _inproj_odd(x, shift, scale, nw, w, conv_w, period):
    bx, lx, d = x.shape
    tm = min(OD_TM, lx)
    nblk = C_WIDTH // OD_TN

    def w_spec(part):
        return pl.BlockSpec((d, OD_TN), lambda b, i, n: (0, part * nblk + n))

    return pl.pallas_call(
        functools.partial(_inproj_odd_kernel, period=period),
        grid=(bx, lx // tm, nblk),
        in_specs=[
            pl.BlockSpec((None, tm, d), lambda b, i, n: (b, i, 0)),
            pl.BlockSpec((None, 1, d), lambda b, i, n: (b, 0, 0)),
            pl.BlockSpec((None, 1, d), lambda b, i, n: (b, 0, 0)),
            pl.BlockSpec((1, d), lambda b, i, n: (0, 0)),
            w_spec(0), w_spec(1), w_spec(2), w_spec(3),
            pl.BlockSpec((3, OD_TN), lambda b, i, n: (0, n)),
        ],
        out_specs=pl.BlockSpec((None, tm, OD_TN), lambda b, i, n: (b, i, n)),
        out_shape=jax.ShapeDtypeStruct((bx, lx, C_WIDTH), BF16),
        scratch_shapes=[pltpu.VMEM((tm, d), BF16)],
        compiler_params=_cparams(("arbitrary", "arbitrary", "arbitrary")),
        name="inproj_odd",
    )(x, shift, scale, nw, w, w, w, w, conv_w)


OUT_TM = 512
OUT_TN = 1024


def _outproj_kernel(*refs, nparts):
    y_refs = refs[:nparts]
    w_refs = refs[nparts:2 * nparts]
    x_ref, g_ref, o_ref = refs[2 * nparts:]
    acc = jnp.dot(y_refs[0][...], w_refs[0][...], preferred_element_type=F32)
    for y_ref, w_ref in zip(y_refs[1:], w_refs[1:]):
        acc = acc + jnp.dot(y_ref[...], w_ref[...], preferred_element_type=F32)
    o_ref[...] = x_ref[...] + g_ref[...] * acc


def _outproj(ys, w, x, gate):
    bx, lx, d = x.shape
    nparts = len(ys)
    kp = w.shape[0] // nparts
    tm = min(OUT_TM, lx)

    def w_spec(part):
        return pl.BlockSpec((kp, OUT_TN), lambda b, i, n: (part, n))

    return pl.pallas_call(
        functools.partial(_outproj_kernel, nparts=nparts),
        grid=(bx, lx // tm, d // OUT_TN),
        in_specs=[pl.BlockSpec((None, tm, kp), lambda b, i, n: (b, i, 0)) for _ in ys]
        + [w_spec(part) for part in range(nparts)]
        + [pl.BlockSpec((None, tm, OUT_TN), lambda b, i, n: (b, i, n)),
           pl.BlockSpec((None, 1, OUT_TN), lambda b, i, n: (b, 0, n))],
        out_specs=pl.BlockSpec((None, tm, OUT_TN), lambda b, i, n: (b, i, n)),
        out_shape=jax.ShapeDtypeStruct((bx, lx, d), F32),
        compiler_params=_cparams(("arbitrary", "arbitrary", "arbitrary")),
        name="outproj",
    )(*ys, *([w] * nparts), x, gate)


FN_TM = 1024


def _final_norm_kernel(x_ref, w_ref, o_ref):
    x = x_ref[...]
    o_ref[...] = (x * lax.rsqrt(jnp.mean(x * x, axis=-1, keepdims=True) + EPS)) * w_ref[...]


def _final_norm(x, w):
    b, seq, d = x.shape
    blk = pl.BlockSpec((None, FN_TM, d), lambda bb, i: (bb, i, 0))
    return pl.pallas_call(
        _final_norm_kernel,
        grid=(b, seq // FN_TM),
        in_specs=[blk, pl.BlockSpec((1, d), lambda bb, i: (0, 0))],
        out_specs=blk,
        out_shape=jax.ShapeDtypeStruct((b, seq, d), F32),
        compiler_params=_cparams(("arbitrary", "arbitrary")),
        name="final_norm",
    )(x, w.reshape(1, d))


def _rope_tables(seq):
    pos = jnp.arange(seq)
    row = (pos // GRID_W).astype(F32)
    col = (pos % GRID_W).astype(F32)
    nf = DK_A // 4
    inv = ROPE_BASE ** (-jnp.arange(nf, dtype=F32) / nf)
    ang = jnp.concatenate([row[:, None] * inv[None], col[:, None] * inv[None]], axis=-1)
    cos, sin = jnp.cos(ang), jnp.sin(ang)
    return jnp.concatenate([cos, cos], axis=-1), jnp.concatenate([-sin, sin], axis=-1)


def _even_layer(x, ctx, mod_x, mod_c, nw, w_in, decay_logit, fno_w, w_out, tables):
    b, seq, d = x.shape
    lc = ctx.shape[1]
    shift_x, scale_x, gate_x = mod_x
    shift_c, scale_c, gate_c = mod_c
    ctx_flat = ctx.reshape(1, b * lc, d)
    cosf, sinf, ones_t, zeros_t = tables
    log_gamma = -jnp.exp(decay_logit.astype(F32))

    px, ux = _inproj_even(x, shift_x, scale_x, nw, w_in, cosf, sinf)
    pc, uc = _inproj_even(ctx_flat, shift_c, scale_c, nw, w_in, ones_t, zeros_t)
    pc = pc.reshape(b, lc, P_WIDTH)
    uc = uc.reshape(b, lc, B_WIDTH)

    zeros_s = jnp.zeros((b, H_A, DK_A, DV_A), F32)
    ya_c, s_f, s_b = _ret(pc, zeros_s, zeros_s, log_gamma)
    ya_x, _, _ = _ret(px, s_f, s_b, log_gamma)

    m12 = _fold_fno(fno_w)
    br, bi = _chan_swap(ux, m12)
    yb_x = _dft_fast(_dft_slow(br, bi), px)
    brc, bic = _chan_dft(uc, m12)
    yb_c = _dft_ctx(brc, bic, pc)

    x = _outproj([ya_x, yb_x], w_out, x, gate_x)
    ctx_flat = _outproj([ya_c.reshape(1, b * lc, A_WIDTH), yb_c.reshape(1, b * lc, B_WIDTH)],
                        w_out, ctx_flat, gate_c)
    return x, ctx_flat.reshape(b, lc, d)


def _odd_layer(x, ctx, mod_x, mod_c, nw, w_in, conv_w, w_out, need_ctx):
    b, seq, d = x.shape
    lc = ctx.shape[1]
    shift_x, scale_x, gate_x = mod_x
    y = _inproj_odd(x, shift_x, scale_x, nw, w_in, conv_w, GRID_W)
    x = _outproj([y], w_out, x, gate_x)
    if need_ctx:
        shift_c, scale_c, gate_c = mod_c
        ctx_flat = ctx.reshape(1, b * lc, d)
        yc = _inproj_odd(ctx_flat, shift_c, scale_c, nw, w_in, conv_w, lc)
        ctx = _outproj([yc], w_out, ctx_flat, gate_c).reshape(b, lc, d)
    return x, ctx


def kernel(x, c, ctx, c_ctx, ada_w, ada_b, norm_w, ev_w_in, ret_decay_logit, fno_w, ev_w_out,
           od_w_in, conv_w, od_w_out, final_norm_w):
    b, seq, d = x.shape
    lc = ctx.shape[1]
    depth = ada_w.shape[0]
    assert seq == DFT_RADIX * DFT_RADIX and seq % GRID_W == 0 and d == D_MODEL

    rows = 16
    s = jnp.zeros((rows, d), F32).at[:b].set(c).at[b].set(c_ctx)
    mod = _ada_mod(s, ada_w, ada_b)

    ev_w_in = ev_w_in.astype(BF16)
    ev_w_out = ev_w_out.astype(BF16)
    od_w_in = od_w_in.astype(BF16)
    od_w_out = od_w_out.astype(BF16)

    cosf, sinf = _rope_tables(seq)
    tables = (cosf, sinf, jnp.ones((b * lc, DK_A), F32), jnp.zeros((b * lc, DK_A), F32))

    for i in range(depth):
        need_ctx = i < depth - 1
        mod_x = tuple(mod[i, :b, k * d:(k + 1) * d][:, None, :] for k in range(3))
        mod_c = tuple(mod[i, b:b + 1, k * d:(k + 1) * d][:, None, :] for k in range(3))
        nw = norm_w[i].reshape(1, d)
        j = i // 2
        if i % 2 == 0:
            x, ctx = _even_layer(x, ctx, mod_x, mod_c, nw, ev_w_in[j], ret_decay_logit[j],
                                 fno_w[j], ev_w_out[j], tables)
        else:
            x, ctx = _odd_layer(x, ctx, mod_x, mod_c, nw, od_w_in[j], conv_w[j], od_w_out[j],
                                need_ctx)
    return _final_norm(x, final_norm_w)
```

```python
import functools

import numpy as np
import jax
import jax.numpy as jnp
from jax import lax
from jax.experimental import pallas as pl
from jax.experimental.pallas import tpu as pltpu

F32 = jnp.float32
BF16 = jnp.bfloat16

D_MODEL = 2048
DEPTH = 4
GRID_W = 64
H_A = 8
DK_A = 128
DV_A = 256
QK_WIDTH = H_A * DK_A
A_WIDTH = H_A * DV_A
G_B = 8
GB_DIM = 256
B_WIDTH = G_B * GB_DIM
EVEN_IN = 2 * QK_WIDTH + 2 * A_WIDTH + 2 * B_WIDTH
C_WIDTH = 4096
ROPE_BASE = 10000.0
EPS = 1e-6
RET_CHUNK = 256
DFT_RADIX = 64

P_WIDTH = 2 * QK_WIDTH + 2 * A_WIDTH + B_WIDTH
P_V0 = 2 * QK_WIDTH
P_GA0 = P_V0 + A_WIDTH
P_GB0 = P_GA0 + A_WIDTH

VMEM_LIMIT = 56 * 1024 * 1024


def _cparams(sem):
    return pltpu.CompilerParams(dimension_semantics=sem, vmem_limit_bytes=VMEM_LIMIT)


def _silu(x):
    return x * (1.0 / (1.0 + jnp.exp(-x)))


def _norm_mod(x, nw, sh, sc):
    y = x * lax.rsqrt(jnp.mean(x * x, axis=-1, keepdims=True) + EPS)
    return (y * nw) * (1.0 + sc) + sh


ADA_TN = 768


def _ada_kernel(s_ref, w_ref, b_ref, o_ref):
    s = _silu(s_ref[...]).astype(BF16)
    w = w_ref[...].astype(BF16)
    o_ref[...] = jnp.dot(s, w, preferred_element_type=F32) + b_ref[...]


def _ada_mod(s, ada_w, ada_b):
    depth, d, n3 = ada_w.shape
    rows = s.shape[0]
    return pl.pallas_call(
        _ada_kernel,
        grid=(depth, n3 // ADA_TN),
        in_specs=[
            pl.BlockSpec((rows, d), lambda l, n: (0, 0)),
            pl.BlockSpec((None, d, ADA_TN), lambda l, n: (l, 0, n)),
            pl.BlockSpec((None, 1, ADA_TN), lambda l, n: (l, 0, n)),
        ],
        out_specs=pl.BlockSpec((None, rows, ADA_TN), lambda l, n: (l, 0, n)),
        out_shape=jax.ShapeDtypeStruct((depth, rows, n3), F32),
        compiler_params=_cparams(("arbitrary", "arbitrary")),
        name="ada_mod",
    )(s, ada_w, ada_b.reshape(depth, 1, n3))


EV_TM = 1024
EV_TN = 1024
EV_NQK = 2 * QK_WIDTH // EV_TN
EV_NV1 = (P_GA0) // EV_TN
EV_NGA1 = (P_GB0) // EV_TN
EV_NU1 = EV_NGA1 + B_WIDTH // EV_TN
EV_NT = EVEN_IN // EV_TN


def _inproj_even_kernel(x_ref, sh_ref, sc_ref, nw_ref, w_ref, cos_ref, sin_ref,
                        p_ref, u_ref, h_ref, *, kscale):
    n = pl.program_id(2)

    @pl.when(n == 0)
    def _():
        h_ref[...] = _norm_mod(x_ref[...], nw_ref[...], sh_ref[...], sc_ref[...]).astype(BF16)

    def project():
        return jnp.dot(h_ref[...], w_ref[...], preferred_element_type=F32)

    @pl.when(n < EV_NQK)
    def _():
        acc = project()
        is_k = n >= (QK_WIDTH // EV_TN)
        scale = jnp.where(is_k, jnp.float32(kscale), jnp.float32(1.0))
        cosf = cos_ref[...]
        sinf = sin_ref[...]
        for hh in range(EV_TN // DK_A):
            t = acc[:, hh * DK_A:(hh + 1) * DK_A]
            rot = pltpu.roll(t, DK_A // 2, 1)
            p_ref[:, hh * DK_A:(hh + 1) * DK_A] = ((t * cosf + rot * sinf) * scale).astype(BF16)

    @pl.when(jnp.logical_and(n >= EV_NQK, n < EV_NV1))
    def _():
        p_ref[...] = project().astype(BF16)

    @pl.when(jnp.logical_or(jnp.logical_and(n >= EV_NV1, n < EV_NGA1), n >= EV_NU1))
    def _():
        p_ref[...] = _silu(project()).astype(BF16)

    @pl.when(jnp.logical_and(n >= EV_NGA1, n < EV_NU1))
    def _():
        u_ref[...] = project().astype(BF16)


def _inproj_even(x, shift, scale, nw, w, layer, cosf, sinf):
    bx, lx, d = x.shape
    tm = min(EV_TM, lx)
    n_u = B_WIDTH // EV_TN

    def p_map(b, i, n):
        return (b, i, jnp.where(n < EV_NGA1, n, jnp.where(n < EV_NU1, EV_NGA1 - 1, n - n_u)))

    def u_map(b, i, n):
        return (b, i, jnp.clip(n - EV_NGA1, 0, n_u - 1))

    return pl.pallas_call(
        functools.partial(_inproj_even_kernel, kscale=DK_A ** -0.5),
        grid=(bx, lx // tm, EV_NT),
        in_specs=[
            pl.BlockSpec((None, tm, d), lambda b, i, n: (b, i, 0)),
            pl.BlockSpec((None, 1, d), lambda b, i, n: (b, 0, 0)),
            pl.BlockSpec((None, 1, d), lambda b, i, n: (b, 0, 0)),
            pl.BlockSpec((1, d), lambda b, i, n: (0, 0)),
            pl.BlockSpec((None, d, EV_TN), lambda b, i, n: (layer, 0, n)),
            pl.BlockSpec((tm, DK_A), lambda b, i, n: (i, 0)),
            pl.BlockSpec((tm, DK_A), lambda b, i, n: (i, 0)),
        ],
        out_specs=[
            pl.BlockSpec((None, tm, EV_TN), p_map),
            pl.BlockSpec((None, tm, EV_TN), u_map),
        ],
        out_shape=[
            jax.ShapeDtypeStruct((bx, lx, P_WIDTH), BF16),
            jax.ShapeDtypeStruct((bx, lx, B_WIDTH), BF16),
        ],
        scratch_shapes=[pltpu.VMEM((tm, d), BF16)],
        compiler_params=_cparams(("arbitrary", "arbitrary", "arbitrary")),
        name="inproj_even",
    )(x, shift, scale, nw, w, cosf, sinf)


def _ret_kernel(lg_ref, q_ref, k_ref, v_ref, g_ref, s0f_ref, s0b_ref,
                y_ref, sf_ref, sb_ref, o_scr, *, seq, chunk):
    h = pl.program_id(1)
    lgf = lg_ref[0, h]
    lgb = lg_ref[1, h]
    nc = seq // chunk
    cf = float(chunk)

    ii = lax.broadcasted_iota(jnp.int32, (chunk, chunk), 0)
    jj = lax.broadcasted_iota(jnp.int32, (chunk, chunk), 1)
    rel = (ii - jj).astype(F32)
    dmat = jnp.where(rel >= 0.0,
                     jnp.exp(lgf * jnp.maximum(rel, 0.0)),
                     jnp.exp(lgb * jnp.maximum(-rel, 0.0)))
    row = lax.broadcasted_iota(jnp.int32, (chunk, 1), 0).astype(F32)
    xi_f = jnp.exp(lgf * (row + 1.0))
    zeta_f = jnp.exp(lgf * (cf - 1.0 - row))
    xi_b = jnp.exp(lgb * (cf - row))
    zeta_b = jnp.exp(lgb * row)
    dec_f = jnp.exp(jnp.full((1, DV_A), lgf * cf, F32))
    dec_b = jnp.exp(jnp.full((1, DV_A), lgb * cf, F32))

    sf_ref[...] = s0f_ref[...]
    sb_ref[...] = s0b_ref[...]

    def rows(c):
        return pl.ds(pl.multiple_of(c * chunk, chunk), chunk)

    def forward(c):
        qn, kn, vn = q_ref[rows(c), :], k_ref[rows(c), :], v_ref[rows(c), :]
        s = sf_ref[...]
        sc = lax.dot_general(qn, kn, (((1,), (1,)), ((), ())), preferred_element_type=F32)
        o = jnp.dot((sc * dmat).astype(BF16), vn, preferred_element_type=F32)
        o = o + jnp.dot(qn, s.astype(BF16), preferred_element_type=F32) * xi_f
        kz = (kn.astype(F32) * zeta_f).astype(BF16)
        sf_ref[...] = dec_f * s + lax.dot_general(kz, vn, (((0,), (0,)), ((), ())),
                                                  preferred_element_type=F32)
        return o

    def backward(c):
        qn, kn, vn = q_ref[rows(c), :], k_ref[rows(c), :], v_ref[rows(c), :]
        s = sb_ref[...]
        o = jnp.dot(qn, s.astype(BF16), preferred_element_type=F32) * xi_b
        kz = (kn.astype(F32) * zeta_b).astype(BF16)
        sb_ref[...] = dec_b * s + lax.dot_general(kz, vn, (((0,), (0,)), ((), ())),
                                                  preferred_element_type=F32)
        return o

    def finish(c, o):
        o = o * lax.rsqrt(jnp.mean(o * o, axis=-1, keepdims=True) + EPS)
        y_ref[rows(c), :] = (o * g_ref[rows(c), :].astype(F32)).astype(BF16)

    if nc == 1:
        finish(0, forward(0) + backward(0))
    else:
        def first_half(t, carry):
            o_scr[rows(t), :] = forward(t)
            o_scr[rows(nc - 1 - t), :] = backward(nc - 1 - t)
            return carry

        def second_half(t, carry):
            finish(t, forward(t) + o_scr[rows(t), :])
            finish(nc - 1 - t, backward(nc - 1 - t) + o_scr[rows(nc - 1 - t), :])
            return carry

        lax.fori_loop(0, nc // 2, first_half, 0)
        lax.fori_loop(nc // 2, nc, second_half, 0)


def _ret(p, s0f, s0b, log_gamma):
    b, seq, _ = p.shape
    chunk = min(RET_CHUNK, seq)
    assert seq % chunk == 0 and (seq == chunk or (seq // chunk) % 2 == 0)
    st_spec = pl.BlockSpec((None, None, DK_A, DV_A), lambda bb, hh: (bb, hh, 0, 0))
    st_shape = jax.ShapeDtypeStruct((b, H_A, DK_A, DV_A), F32)
    return pl.pallas_call(
        functools.partial(_ret_kernel, seq=seq, chunk=chunk),
        grid=(b, H_A),
        in_specs=[
            pl.BlockSpec(memory_space=pltpu.SMEM),
            pl.BlockSpec((None, seq, DK_A), lambda bb, hh: (bb, 0, hh)),
            pl.BlockSpec((None, seq, DK_A), lambda bb, hh: (bb, 0, QK_WIDTH // DK_A + hh)),
            pl.BlockSpec((None, seq, DV_A), lambda bb, hh: (bb, 0, P_V0 // DV_A + hh)),
            pl.BlockSpec((None, seq, DV_A), lambda bb, hh: (bb, 0, P_GA0 // DV_A + hh)),
            st_spec, st_spec,
        ],
        out_specs=[
            pl.BlockSpec((None, seq, DV_A), lambda bb, hh: (bb, 0, hh)),
            st_spec, st_spec,
        ],
        out_shape=[jax.ShapeDtypeStruct((b, seq, A_WIDTH), BF16), st_shape, st_shape],
        scratch_shapes=[pltpu.VMEM((seq, DV_A), F32)],
        compiler_params=_cparams(("arbitrary", "arbitrary")),
        name="ret",
    )(log_gamma, p, p, p, p, s0f, s0b)


def _dft_cos_sin(n):
    kk = (np.arange(n)[:, None] * np.arange(n)[None, :]) % n
    ang = 2.0 * np.pi * kk / n
    return np.cos(ang), np.sin(ang)


def _fold_kernel(cc_ref, sc_ref, w_ref, o_ref):
    w = w_ref[...]
    o_ref[:, :GB_DIM] = jnp.dot(cc_ref[...], w, precision=lax.Precision.HIGHEST,
                                preferred_element_type=F32).astype(BF16)
    o_ref[:, GB_DIM:] = (-jnp.dot(sc_ref[...], w, precision=lax.Precision.HIGHEST,
                                  preferred_element_type=F32)).astype(BF16)


def _fold_fno(fno_w):
    cc, sc = _dft_cos_sin(GB_DIM)
    mat = pl.BlockSpec((GB_DIM, GB_DIM), lambda g: (0, 0))
    return pl.pallas_call(
        _fold_kernel,
        grid=(G_B,),
        in_specs=[mat, mat, pl.BlockSpec((None, GB_DIM, GB_DIM), lambda g: (g, 0, 0))],
        out_specs=pl.BlockSpec((None, GB_DIM, 2 * GB_DIM), lambda g: (g, 0, 0)),
        out_shape=jax.ShapeDtypeStruct((G_B, GB_DIM, 2 * GB_DIM), BF16),
        compiler_params=_cparams(("arbitrary",)),
        name="fold_fno",
    )(jnp.asarray(cc, F32), jnp.asarray(sc, F32), fno_w)


CH_TM = 2048


def _chan_kernel(u_ref, m_ref, br_ref, bi_ref):
    r = jnp.dot(u_ref[...], m_ref[...], preferred_element_type=F32)
    br_ref[...] = r[:, :GB_DIM].astype(BF16)
    bi_ref[...] = r[:, GB_DIM:].astype(BF16)


def _chan_dft(u, m12):
    bx, lx, _ = u.shape
    tm = min(CH_TM, lx)
    col = pl.BlockSpec((None, tm, GB_DIM), lambda b, i, g: (b, i, g))
    shp = jax.ShapeDtypeStruct((bx, lx, B_WIDTH), BF16)
    return pl.pallas_call(
        _chan_kernel,
        grid=(bx, lx // tm, G_B),
        in_specs=[col, pl.BlockSpec((None, GB_DIM, 2 * GB_DIM), lambda b, i, g: (g, 0, 0))],
        out_specs=[col, col],
        out_shape=[shp, shp],
        compiler_params=_cparams(("arbitrary", "arbitrary", "arbitrary")),
        name="chan_dft",
    )(u, m12)


SWAP_T = 16
SLOW_T = 8


def _chan_swap_kernel(u_ref, m_ref, br_ref, bi_ref):
    r = DFT_RADIX
    x = jnp.swapaxes(u_ref[...], 0, 1).reshape(SWAP_T * r, B_WIDTH)
    for g in range(G_B):
        cols = slice(g * GB_DIM, (g + 1) * GB_DIM)
        y = jnp.dot(x[:, cols], m_ref[g], preferred_element_type=F32)
        br_ref[:, :, cols] = y[:, :GB_DIM].reshape(SWAP_T, r, GB_DIM).astype(BF16)
        bi_ref[:, :, cols] = y[:, GB_DIM:].reshape(SWAP_T, r, GB_DIM).astype(BF16)


def _chan_swap(u, m12):
    b = u.shape[0]
    r = DFT_RADIX
    shp = jax.ShapeDtypeStruct((b, r, r, B_WIDTH), BF16)
    out = pl.BlockSpec((None, SWAP_T, r, B_WIDTH), lambda bb, j: (bb, j, 0, 0))
    return pl.pallas_call(
        _chan_swap_kernel,
        grid=(b, r // SWAP_T),
        in_specs=[pl.BlockSpec((None, r, SWAP_T, B_WIDTH), lambda bb, j: (bb, 0, j, 0)),
                  pl.BlockSpec((G_B, GB_DIM, 2 * GB_DIM), lambda bb, j: (0, 0, 0))],
        out_specs=[out, out],
        out_shape=[shp, shp],
        compiler_params=_cparams(("arbitrary", "arbitrary")),
        name="chan_swap",
    )(u.reshape(b, r, r, B_WIDTH), m12)


def _slow_matrices():
    n = DFT_RADIX * DFT_RADIX
    bb = np.arange(DFT_RADIX)[:, None, None]
    k1 = np.arange(DFT_RADIX)[None, :, None]
    a = np.arange(DFT_RADIX)[None, None, :]
    ang = 2.0 * np.pi * ((DFT_RADIX * a * k1 + bb * k1) % n) / n
    c, s = np.cos(ang), np.sin(ang)
    return np.concatenate([np.concatenate([c, s], axis=2), np.concatenate([-s, c], axis=2)], axis=1)


def _fast_matrix():
    c, s = _dft_cos_sin(DFT_RADIX)
    return np.concatenate([c, s], axis=1)


def _slow_kernel(w_ref, xr_ref, xi_ref, y_ref):
    for bl in range(SLOW_T):
        x = jnp.concatenate([xr_ref[bl], xi_ref[bl]], axis=0)
        y = jnp.dot(w_ref[bl], x, preferred_element_type=F32)
        y_ref[0, bl] = y[:DFT_RADIX].astype(BF16)
        y_ref[1, bl] = y[DFT_RADIX:].astype(BF16)


def _dft_slow(br, bi):
    b = br.shape[0]
    r = DFT_RADIX
    xin = pl.BlockSpec((None, SLOW_T, r, B_WIDTH), lambda bb, j: (bb, j, 0, 0))
    return pl.pallas_call(
        _slow_kernel,
        grid=(b, r // SLOW_T),
        in_specs=[pl.BlockSpec((SLOW_T, 2 * r, 2 * r), lambda bb, j: (j, 0, 0)), xin, xin],
        out_specs=pl.BlockSpec((None, 2, SLOW_T, r, B_WIDTH), lambda bb, j: (bb, 0, j, 0, 0)),
        out_shape=jax.ShapeDtypeStruct((b, 2, r, r, B_WIDTH), BF16),
        compiler_params=_cparams(("arbitrary", "arbitrary")),
        name="dft_slow",
    )(jnp.asarray(_slow_matrices(), BF16), br, bi)


def _fast_kernel(w_ref, y_ref, g_ref, o_ref, o_scr, *, scale):
    zr = jnp.swapaxes(y_ref[0], 0, 1)
    zi = jnp.swapaxes(y_ref[1], 0, 1)
    w = w_ref[...]
    for kl in range(SWAP_T):
        z = jnp.concatenate([zr[kl], zi[kl]], axis=0)
        o_scr[kl] = jnp.dot(w, z, preferred_element_type=F32).astype(BF16)
    o = jnp.swapaxes(o_scr[...], 0, 1).astype(F32)
    o_ref[...] = ((o * scale) * g_ref[...].astype(F32)).astype(BF16)


def _dft_fast(y, p):
    b = y.shape[0]
    r = DFT_RADIX
    seq = r * r
    blk = pl.BlockSpec((None, r, SWAP_T, B_WIDTH), lambda bb, j: (bb, 0, j, 0))
    out = pl.pallas_call(
        functools.partial(_fast_kernel, scale=float((seq * GB_DIM) ** -0.5)),
        grid=(b, r // SWAP_T),
        in_specs=[
            pl.BlockSpec((r, 2 * r), lambda bb, j: (0, 0)),
            pl.BlockSpec((None, 2, r, SWAP_T, B_WIDTH), lambda bb, j: (bb, 0, 0, j, 0)),
            pl.BlockSpec((None, r, SWAP_T, B_WIDTH), lambda bb, j: (bb, 0, j, P_GB0 // B_WIDTH)),
        ],
        out_specs=blk,
        out_shape=jax.ShapeDtypeStruct((b, r, r, B_WIDTH), BF16),
        scratch_shapes=[pltpu.VMEM((SWAP_T, r, B_WIDTH), BF16)],
        compiler_params=_cparams(("arbitrary", "arbitrary")),
        name="dft_fast",
    )(jnp.asarray(_fast_matrix(), BF16), y, p.reshape(b, r, r, P_WIDTH))
    return out.reshape(b, seq, B_WIDTH)


def _ctx_dft_kernel(w_ref, xr_ref, xi_ref, g_ref, o_ref, *, scale):
    x = jnp.concatenate([xr_ref[...], xi_ref[...]], axis=0)
    o = jnp.dot(w_ref[...], x, preferred_element_type=F32)
    o_ref[...] = ((o * scale) * g_ref[...].astype(F32)).astype(BF16)


def _dft_ctx(br, bi, p):
    b, seq, _ = br.shape
    c, s = _dft_cos_sin(seq)
    w = jnp.asarray(np.concatenate([c, s], axis=1), BF16)
    blk = pl.BlockSpec((None, seq, B_WIDTH), lambda bb: (bb, 0, 0))
    return pl.pallas_call(
        functools.partial(_ctx_dft_kernel, scale=float((seq * GB_DIM) ** -0.5)),
        grid=(b,),
        in_specs=[pl.BlockSpec((seq, 2 * seq), lambda bb: (0, 0)), blk, blk,
                  pl.BlockSpec((None, seq, B_WIDTH), lambda bb: (bb, 0, P_GB0 // B_WIDTH))],
        out_specs=blk,
        out_shape=jax.ShapeDtypeStruct((b, seq, B_WIDTH), BF16),
        compiler_params=_cparams(("arbitrary",)),
        name="dft_ctx",
    )(w, br, bi, p)


OD_TM = 512
OD_TN = 512


def _inproj_odd_kernel(x_ref, sh_ref, sc_ref, nw_ref, wb_ref, wc_ref, wx_ref, wg_ref, cw_ref,
                       y_ref, h_ref, *, period):
    @pl.when(pl.program_id(2) == 0)
    def _():
        h_ref[...] = _norm_mod(x_ref[...], nw_ref[...], sh_ref[...], sc_ref[...]).astype(BF16)

    h = h_ref[...]
    tm = h.shape[0]
    z = (jnp.dot(h, wc_ref[...], preferred_element_type=F32)
         * jnp.dot(h, wx_ref[...], preferred_element_type=F32))
    pos = lax.broadcasted_iota(jnp.int32, (tm, 1), 0) % period
    z_prev = jnp.where(pos == 0, 0.0, pltpu.roll(z, 1, 0))
    z_next = jnp.where(pos == period - 1, 0.0, pltpu.roll(z, tm - 1, 0))
    cw = cw_ref[...]
    conv = z_prev * cw[0:1, :] + z * cw[1:2, :] + z_next * cw[2:3, :]
    bg = jnp.dot(h, wb_ref[...], preferred_element_type=F32)
    g = jnp.dot(h, wg_ref[...], preferred_element_type=F32)
    y_ref[...] = (bg * conv * _silu(g)).astype(BF16)


def _inproj_odd(x, shift, scale, nw, w, conv_w, layer, period):
    bx, lx, d = x.shape
    tm = min(OD_TM, lx)
    nblk = C_WIDTH // OD_TN

    def w_spec(part):
        return pl.BlockSpec((None, d, OD_TN), lambda b, i, n: (layer, 0, part * nblk + n))

    return pl.pallas_call(
        functools.partial(_inproj_odd_kernel, period=period),
        grid=(bx, lx // tm, nblk),
        in_specs=[
            pl.BlockSpec((None, tm, d), lambda b, i, n: (b, i, 0)),
            pl.BlockSpec((None, 1, d), lambda b, i, n: (b, 0, 0)),
            pl.BlockSpec((None, 1, d), lambda b, i, n: (b, 0, 0)),
            pl.BlockSpec((1, d), lambda b, i, n: (0, 0)),
            w_spec(0), w_spec(1), w_spec(2), w_spec(3),
            pl.BlockSpec((None, 3, OD_TN), lambda b, i, n: (layer, 0, n)),
        ],
        out_specs=pl.BlockSpec((None, tm, OD_TN), lambda b, i, n: (b, i, n)),
        out_shape=jax.ShapeDtypeStruct((bx, lx, C_WIDTH), BF16),
        scratch_shapes=[pltpu.VMEM((tm, d), BF16)],
        compiler_params=_cparams(("arbitrary", "arbitrary", "arbitrary")),
        name="inproj_odd",
    )(x, shift, scale, nw, w, w, w, w, conv_w)


OUT_TM = 512


def _outproj_kernel(*refs, nparts, final):
    y_refs = refs[:nparts]
    w_refs = refs[nparts:2 * nparts]
    x_ref, g_ref, fw_ref, o_ref = refs[2 * nparts:]
    acc = jnp.dot(y_refs[0][...], w_refs[0][...], preferred_element_type=F32)
    for y_ref, w_ref in zip(y_refs[1:], w_refs[1:]):
        acc = acc + jnp.dot(y_ref[...], w_ref[...], preferred_element_type=F32)
    o = x_ref[...] + g_ref[...] * acc
    if final:
        o = (o * lax.rsqrt(jnp.mean(o * o, axis=-1, keepdims=True) + EPS)) * fw_ref[...]
    o_ref[...] = o


def _outproj(ys, w, layer, x, gate, fw, final):
    bx, lx, d = x.shape
    nparts = len(ys)
    kp = w.shape[1] // nparts
    tm = min(OUT_TM, lx)

    def w_spec(part):
        return pl.BlockSpec((None, kp, d), lambda b, i: (layer, part, 0),
                            pipeline_mode=pl.Buffered(1))

    row = pl.BlockSpec((None, tm, d), lambda b, i: (b, i, 0))
    return pl.pallas_call(
        functools.partial(_outproj_kernel, nparts=nparts, final=final),
        grid=(bx, lx // tm),
        in_specs=[pl.BlockSpec((None, tm, kp), lambda b, i: (b, i, 0)) for _ in ys]
        + [w_spec(part) for part in range(nparts)]
        + [row, pl.BlockSpec((None, 1, d), lambda b, i: (b, 0, 0)),
           pl.BlockSpec((1, d), lambda b, i: (0, 0))],
        out_specs=row,
        out_shape=jax.ShapeDtypeStruct((bx, lx, d), F32),
        compiler_params=_cparams(("arbitrary", "arbitrary")),
        name="outproj",
    )(*ys, *([w] * nparts), x, gate, fw)


def _rope_tables(seq):
    pos = jnp.arange(seq)
    row = (pos // GRID_W).astype(F32)
    col = (pos % GRID_W).astype(F32)
    nf = DK_A // 4
    inv = ROPE_BASE ** (-jnp.arange(nf, dtype=F32) / nf)
    ang = jnp.concatenate([row[:, None] * inv[None], col[:, None] * inv[None]], axis=-1)
    cos, sin = jnp.cos(ang), jnp.sin(ang)
    return jnp.concatenate([cos, cos], axis=-1), jnp.concatenate([-sin, sin], axis=-1)


def _even_layer(x, ctx, mod_x, mod_c, nw, w_in, layer, decay_logit, fno_w, w_out, tables, fw):
    b, seq, d = x.shape
    lc = ctx.shape[1]
    shift_x, scale_x, gate_x = mod_x
    shift_c, scale_c, gate_c = mod_c
    ctx_flat = ctx.reshape(1, b * lc, d)
    cosf, sinf, ones_t, zeros_t = tables
    log_gamma = -jnp.exp(decay_logit.astype(F32))

    px, ux = _inproj_even(x, shift_x, scale_x, nw, w_in, layer, cosf, sinf)
    pc, uc = _inproj_even(ctx_flat, shift_c, scale_c, nw, w_in, layer, ones_t, zeros_t)
    pc = pc.reshape(b, lc, P_WIDTH)
    uc = uc.reshape(b, lc, B_WIDTH)

    zeros_s = jnp.zeros((b, H_A, DK_A, DV_A), F32)
    ya_c, s_f, s_b = _ret(pc, zeros_s, zeros_s, log_gamma)
    ya_x, _, _ = _ret(px, s_f, s_b, log_gamma)

    m12 = _fold_fno(fno_w)
    br, bi = _chan_swap(ux, m12)
    yb_x = _dft_fast(_dft_slow(br, bi), px)
    brc, bic = _chan_dft(uc, m12)
    yb_c = _dft_ctx(brc, bic, pc)

    x = _outproj([ya_x, yb_x], w_out, layer, x, gate_x, fw, False)
    ctx_flat = _outproj([ya_c.reshape(1, b * lc, A_WIDTH), yb_c.reshape(1, b * lc, B_WIDTH)],
                        w_out, layer, ctx_flat, gate_c, fw, False)
    return x, ctx_flat.reshape(b, lc, d)


def _odd_layer(x, ctx, mod_x, mod_c, nw, w_in, conv_w, w_out, layer, need_ctx, fw, final):
    b, seq, d = x.shape
    lc = ctx.shape[1]
    shift_x, scale_x, gate_x = mod_x
    y = _inproj_odd(x, shift_x, scale_x, nw, w_in, conv_w, layer, GRID_W)
    x = _outproj([y], w_out, layer, x, gate_x, fw, final)
    if need_ctx:
        shift_c, scale_c, gate_c = mod_c
        ctx_flat = ctx.reshape(1, b * lc, d)
        yc = _inproj_odd(ctx_flat, shift_c, scale_c, nw, w_in, conv_w, layer, lc)
        ctx = _outproj([yc], w_out, layer, ctx_flat, gate_c, fw, False).reshape(b, lc, d)
    return x, ctx


def kernel(x, c, ctx, c_ctx, ada_w, ada_b, norm_w, ev_w_in, ret_decay_logit, fno_w, ev_w_out,
           od_w_in, conv_w, od_w_out, final_norm_w):
    b, seq, d = x.shape
    lc = ctx.shape[1]
    depth = ada_w.shape[0]
    assert seq == DFT_RADIX * DFT_RADIX and seq % GRID_W == 0 and d == D_MODEL
    assert depth % 2 == 0

    rows = 16
    s = jnp.zeros((rows, d), F32).at[:b].set(c).at[b].set(c_ctx)
    mod = _ada_mod(s, ada_w, ada_b)

    ev_w_in = ev_w_in.astype(BF16)
    ev_w_out = ev_w_out.astype(BF16)
    od_w_in = od_w_in.astype(BF16)
    od_w_out = od_w_out.astype(BF16)

    fw = final_norm_w.reshape(1, d)
    cosf, sinf = _rope_tables(seq)
    tables = (cosf, sinf, jnp.ones((b * lc, DK_A), F32), jnp.zeros((b * lc, DK_A), F32))

    for i in range(depth):
        need_ctx = i < depth - 1
        mod_x = tuple(mod[i, :b, k * d:(k + 1) * d][:, None, :] for k in range(3))
        mod_c = tuple(mod[i, b:b + 1, k * d:(k + 1) * d][:, None, :] for k in range(3))
        nw = norm_w[i].reshape(1, d)
        j = i // 2
        if i % 2 == 0:
            x, ctx = _even_layer(x, ctx, mod_x, mod_c, nw, ev_w_in, j, ret_decay_logit[j],
                                 fno_w[j], ev_w_out, tables, fw)
        else:
            x, ctx = _odd_layer(x, ctx, mod_x, mod_c, nw, od_w_in, conv_w, od_w_out, j,
                                need_ctx, fw, i == depth - 1)
    return x
```

```python
import functools

import numpy as np
import jax
import jax.numpy as jnp
from jax import lax
from jax.experimental import pallas as pl
from jax.experimental.pallas import tpu as pltpu

F32 = jnp.float32
BF16 = jnp.bfloat16

D_MODEL = 2048
DEPTH = 4
GRID_W = 64
H_A = 8
DK_A = 128
DV_A = 256
QK_WIDTH = H_A * DK_A
A_WIDTH = H_A * DV_A
G_B = 8
GB_DIM = 256
B_WIDTH = G_B * GB_DIM
EVEN_IN = 2 * QK_WIDTH + 2 * A_WIDTH + 2 * B_WIDTH
C_WIDTH = 4096
ROPE_BASE = 10000.0
EPS = 1e-6
RET_CHUNK = 256
DFT_RADIX = 64

P_WIDTH = 2 * QK_WIDTH + 2 * A_WIDTH + B_WIDTH
P_V0 = 2 * QK_WIDTH
P_GA0 = P_V0 + A_WIDTH
P_GB0 = P_GA0 + A_WIDTH

VMEM_LIMIT = 56 * 1024 * 1024


def _cparams(sem):
    return pltpu.CompilerParams(dimension_semantics=sem, vmem_limit_bytes=VMEM_LIMIT)


def _silu(x):
    return x * (1.0 / (1.0 + jnp.exp(-x)))


def _norm_mod(x, nw, sh, sc):
    y = x * lax.rsqrt(jnp.mean(x * x, axis=-1, keepdims=True) + EPS)
    return (y * nw) * (1.0 + sc) + sh


ADA_TN = 768


def _ada_kernel(s_ref, w_ref, b_ref, o_ref):
    s = _silu(s_ref[...]).astype(BF16)
    w = w_ref[...].astype(BF16)
    o_ref[...] = jnp.dot(s, w, preferred_element_type=F32) + b_ref[...]


def _ada_mod(s, ada_w, ada_b):
    depth, d, n3 = ada_w.shape
    rows = s.shape[0]
    return pl.pallas_call(
        _ada_kernel,
        grid=(depth, n3 // ADA_TN),
        in_specs=[
            pl.BlockSpec((rows, d), lambda l, n: (0, 0)),
            pl.BlockSpec((None, d, ADA_TN), lambda l, n: (l, 0, n)),
            pl.BlockSpec((None, 1, ADA_TN), lambda l, n: (l, 0, n)),
        ],
        out_specs=pl.BlockSpec((None, rows, ADA_TN), lambda l, n: (l, 0, n)),
        out_shape=jax.ShapeDtypeStruct((depth, rows, n3), F32),
        compiler_params=_cparams(("arbitrary", "arbitrary")),
        name="ada_mod",
    )(s, ada_w, ada_b.reshape(depth, 1, n3))


EV_TM = 1024
EV_TN = 1024
EV_NQK = 2 * QK_WIDTH // EV_TN
EV_NV1 = (P_GA0) // EV_TN
EV_NGA1 = (P_GB0) // EV_TN
EV_NU1 = EV_NGA1 + B_WIDTH // EV_TN
EV_NT = EVEN_IN // EV_TN
EV_AHEAD = EV_NQK


def _ahead_maps(n_i, n_tiles, ahead_step):
    def tile(b, i, n):
        return jnp.minimum(b * n_i + i + jnp.where(n >= ahead_step, 1, 0), n_tiles - 1)

    def x_map(b, i, n):
        t = tile(b, i, n)
        return (t // n_i, t % n_i, 0)

    def mod_map(b, i, n):
        return (tile(b, i, n) // n_i, 0, 0)

    return x_map, mod_map


def _inproj_even_kernel(x_ref, sh_ref, sc_ref, nw_ref, w_ref, cos_ref, sin_ref,
                        p_ref, u_ref, h_scr, *, kscale, n_i):
    n = pl.program_id(2)
    tile = pl.program_id(0) * n_i + pl.program_id(1)
    slot = tile % 2
    tm = x_ref.shape[0]

    def norm_rows(rows):
        return _norm_mod(x_ref[rows, :], nw_ref[...], sh_ref[...], sc_ref[...]).astype(BF16)

    @pl.when(jnp.logical_and(tile == 0, n == 0))
    def _():
        h_scr[0] = norm_rows(slice(None))

    def project():
        return jnp.dot(h_scr[slot], w_ref[...], preferred_element_type=F32)

    ahead_rows = tm // (EV_NT - EV_AHEAD)

    def norm_ahead():
        rows = pl.ds(pl.multiple_of((n - EV_AHEAD) * ahead_rows, ahead_rows), ahead_rows)
        h_scr[1 - slot, rows, :] = norm_rows(rows)

    @pl.when(n < EV_NQK)
    def _():
        acc = project()
        is_k = n >= (QK_WIDTH // EV_TN)
        scale = jnp.where(is_k, jnp.float32(kscale), jnp.float32(1.0))
        cosf = cos_ref[...]
        sinf = sin_ref[...]
        for hh in range(EV_TN // DK_A):
            t = acc[:, hh * DK_A:(hh + 1) * DK_A]
            rot = pltpu.roll(t, DK_A // 2, 1)
            p_ref[:, hh * DK_A:(hh + 1) * DK_A] = ((t * cosf + rot * sinf) * scale).astype(BF16)

    @pl.when(jnp.logical_and(n >= EV_NQK, n < EV_NV1))
    def _():
        p_ref[...] = project().astype(BF16)
        norm_ahead()

    @pl.when(jnp.logical_or(jnp.logical_and(n >= EV_NV1, n < EV_NGA1), n >= EV_NU1))
    def _():
        p_ref[...] = _silu(project()).astype(BF16)
        norm_ahead()

    @pl.when(jnp.logical_and(n >= EV_NGA1, n < EV_NU1))
    def _():
        u_ref[...] = project().astype(BF16)
        norm_ahead()


def _inproj_even(x, shift, scale, nw, w, layer, cosf, sinf):
    bx, lx, d = x.shape
    tm = min(EV_TM, lx)
    n_u = B_WIDTH // EV_TN

    def p_map(b, i, n):
        return (b, i, jnp.where(n < EV_NGA1, n, jnp.where(n < EV_NU1, EV_NGA1 - 1, n - n_u)))

    def u_map(b, i, n):
        return (b, i, jnp.clip(n - EV_NGA1, 0, n_u - 1))

    n_i = lx // tm
    x_map, mod_map = _ahead_maps(n_i, bx * n_i, EV_AHEAD)
    return pl.pallas_call(
        functools.partial(_inproj_even_kernel, kscale=DK_A ** -0.5, n_i=n_i),
        grid=(bx, n_i, EV_NT),
        in_specs=[
            pl.BlockSpec((None, tm, d), x_map),
            pl.BlockSpec((None, 1, d), mod_map),
            pl.BlockSpec((None, 1, d), mod_map),
            pl.BlockSpec((1, d), lambda b, i, n: (0, 0)),
            pl.BlockSpec((None, d, EV_TN), lambda b, i, n: (layer, 0, n)),
            pl.BlockSpec((tm, DK_A), lambda b, i, n: (i, 0)),
            pl.BlockSpec((tm, DK_A), lambda b, i, n: (i, 0)),
        ],
        out_specs=[
            pl.BlockSpec((None, tm, EV_TN), p_map),
            pl.BlockSpec((None, tm, EV_TN), u_map),
        ],
        out_shape=[
            jax.ShapeDtypeStruct((bx, lx, P_WIDTH), BF16),
            jax.ShapeDtypeStruct((bx, lx, B_WIDTH), BF16),
        ],
        scratch_shapes=[pltpu.VMEM((2, tm, d), BF16)],
        compiler_params=_cparams(("arbitrary", "arbitrary", "arbitrary")),
        name="inproj_even",
    )(x, shift, scale, nw, w, cosf, sinf)


def _ret_kernel(lg_ref, q_ref, k_ref, v_ref, g_ref, s0f_ref, s0b_ref,
                y_ref, sf_ref, sb_ref, o_scr, *, seq, chunk):
    h = pl.program_id(1)
    lgf = lg_ref[0, h]
    lgb = lg_ref[1, h]
    nc = seq // chunk
    cf = float(chunk)

    ii = lax.broadcasted_iota(jnp.int32, (chunk, chunk), 0)
    jj = lax.broadcasted_iota(jnp.int32, (chunk, chunk), 1)
    rel = (ii - jj).astype(F32)
    dmat = jnp.where(rel >= 0.0,
                     jnp.exp(lgf * jnp.maximum(rel, 0.0)),
                     jnp.exp(lgb * jnp.maximum(-rel, 0.0)))
    row = lax.broadcasted_iota(jnp.int32, (chunk, 1), 0).astype(F32)
    xi_f = jnp.exp(lgf * (row + 1.0))
    zeta_f = jnp.exp(lgf * (cf - 1.0 - row))
    xi_b = jnp.exp(lgb * (cf - row))
    zeta_b = jnp.exp(lgb * row)
    dec_f = jnp.exp(jnp.full((1, DV_A), lgf * cf, F32))
    dec_b = jnp.exp(jnp.full((1, DV_A), lgb * cf, F32))

    sf_ref[...] = s0f_ref[...]
    sb_ref[...] = s0b_ref[...]

    def rows(c):
        return pl.ds(pl.multiple_of(c * chunk, chunk), chunk)

    def forward(c):
        qn, kn, vn = q_ref[rows(c), :], k_ref[rows(c), :], v_ref[rows(c), :]
        s = sf_ref[...]
        sc = lax.dot_general(qn, kn, (((1,), (1,)), ((), ())), preferred_element_type=F32)
        o = jnp.dot((sc * dmat).astype(BF16), vn, preferred_element_type=F32)
        o = o + jnp.dot(qn, s.astype(BF16), preferred_element_type=F32) * xi_f
        kz = (kn.astype(F32) * zeta_f).astype(BF16)
        sf_ref[...] = dec_f * s + lax.dot_general(kz, vn, (((0,), (0,)), ((), ())),
                                                  preferred_element_type=F32)
        return o

    def backward(c):
        qn, kn, vn = q_ref[rows(c), :], k_ref[rows(c), :], v_ref[rows(c), :]
        s = sb_ref[...]
        o = jnp.dot(qn, s.astype(BF16), preferred_element_type=F32) * xi_b
        kz = (kn.astype(F32) * zeta_b).astype(BF16)
        sb_ref[...] = dec_b * s + lax.dot_general(kz, vn, (((0,), (0,)), ((), ())),
                                                  preferred_element_type=F32)
        return o

    def finish(c, o):
        o = o * lax.rsqrt(jnp.mean(o * o, axis=-1, keepdims=True) + EPS)
        y_ref[rows(c), :] = (o * g_ref[rows(c), :].astype(F32)).astype(BF16)

    if nc == 1:
        finish(0, forward(0) + backward(0))
    else:
        def first_half(t, carry):
            o_scr[rows(t), :] = forward(t)
            o_scr[rows(nc - 1 - t), :] = backward(nc - 1 - t)
            return carry

        def second_half(t, carry):
            finish(t, forward(t) + o_scr[rows(t), :])
            finish(nc - 1 - t, backward(nc - 1 - t) + o_scr[rows(nc - 1 - t), :])
            return carry

        lax.fori_loop(0, nc // 2, first_half, 0, unroll=True)
        lax.fori_loop(nc // 2, nc, second_half, 0, unroll=True)


def _ret(p, s0f, s0b, log_gamma):
    b, seq, _ = p.shape
    chunk = min(RET_CHUNK, seq)
    assert seq % chunk == 0 and (seq == chunk or (seq // chunk) % 2 == 0)
    st_spec = pl.BlockSpec((None, None, DK_A, DV_A), lambda bb, hh: (bb, hh, 0, 0))
    st_shape = jax.ShapeDtypeStruct((b, H_A, DK_A, DV_A), F32)
    return pl.pallas_call(
        functools.partial(_ret_kernel, seq=seq, chunk=chunk),
        grid=(b, H_A),
        in_specs=[
            pl.BlockSpec(memory_space=pltpu.SMEM),
            pl.BlockSpec((None, seq, DK_A), lambda bb, hh: (bb, 0, hh)),
            pl.BlockSpec((None, seq, DK_A), lambda bb, hh: (bb, 0, QK_WIDTH // DK_A + hh)),
            pl.BlockSpec((None, seq, DV_A), lambda bb, hh: (bb, 0, P_V0 // DV_A + hh)),
            pl.BlockSpec((None, seq, DV_A), lambda bb, hh: (bb, 0, P_GA0 // DV_A + hh)),
            st_spec, st_spec,
        ],
        out_specs=[
            pl.BlockSpec((None, seq, DV_A), lambda bb, hh: (bb, 0, hh)),
            st_spec, st_spec,
        ],
        out_shape=[jax.ShapeDtypeStruct((b, seq, A_WIDTH), BF16), st_shape, st_shape],
        scratch_shapes=[pltpu.VMEM((seq, DV_A), F32)],
        compiler_params=_cparams(("arbitrary", "arbitrary")),
        name="ret",
    )(log_gamma, p, p, p, p, s0f, s0b)


def _dft_cos_sin(n):
    kk = (np.arange(n)[:, None] * np.arange(n)[None, :]) % n
    ang = 2.0 * np.pi * kk / n
    return np.cos(ang), np.sin(ang)


def _fold_kernel(cc_ref, sc_ref, w_ref, o_ref):
    w = w_ref[...]
    o_ref[:, :GB_DIM] = jnp.dot(cc_ref[...], w, precision=lax.Precision.HIGHEST,
                                preferred_element_type=F32).astype(BF16)
    o_ref[:, GB_DIM:] = (-jnp.dot(sc_ref[...], w, precision=lax.Precision.HIGHEST,
                                  preferred_element_type=F32)).astype(BF16)


def _fold_fno(fno_w):
    cc, sc = _dft_cos_sin(GB_DIM)
    mat = pl.BlockSpec((GB_DIM, GB_DIM), lambda g: (0, 0))
    return pl.pallas_call(
        _fold_kernel,
        grid=(G_B,),
        in_specs=[mat, mat, pl.BlockSpec((None, GB_DIM, GB_DIM), lambda g: (g, 0, 0))],
        out_specs=pl.BlockSpec((None, GB_DIM, 2 * GB_DIM), lambda g: (g, 0, 0)),
        out_shape=jax.ShapeDtypeStruct((G_B, GB_DIM, 2 * GB_DIM), BF16),
        compiler_params=_cparams(("arbitrary",)),
        name="fold_fno",
    )(jnp.asarray(cc, F32), jnp.asarray(sc, F32), fno_w)


CH_TM = 2048


def _chan_kernel(u_ref, m_ref, br_ref, bi_ref):
    r = jnp.dot(u_ref[...], m_ref[...], preferred_element_type=F32)
    br_ref[...] = r[:, :GB_DIM].astype(BF16)
    bi_ref[...] = r[:, GB_DIM:].astype(BF16)


def _chan_dft(u, m12):
    bx, lx, _ = u.shape
    tm = min(CH_TM, lx)
    col = pl.BlockSpec((None, tm, GB_DIM), lambda b, i, g: (b, i, g))
    shp = jax.ShapeDtypeStruct((bx, lx, B_WIDTH), BF16)
    return pl.pallas_call(
        _chan_kernel,
        grid=(bx, lx // tm, G_B),
        in_specs=[col, pl.BlockSpec((None, GB_DIM, 2 * GB_DIM), lambda b, i, g: (g, 0, 0))],
        out_specs=[col, col],
        out_shape=[shp, shp],
        compiler_params=_cparams(("arbitrary", "arbitrary", "arbitrary")),
        name="chan_dft",
    )(u, m12)


SWAP_T = 16


def _chan_slow_kernel(u_ref, m_ref, w_ref, y_ref, br_scr, bi_scr):
    r = DFT_RADIX
    x = jnp.swapaxes(u_ref[...], 0, 1).reshape(SWAP_T * r, B_WIDTH)
    for g in range(G_B):
        cols = slice(g * GB_DIM, (g + 1) * GB_DIM)
        z = jnp.dot(x[:, cols], m_ref[g], preferred_element_type=F32)
        br_scr[:, :, cols] = z[:, :GB_DIM].reshape(SWAP_T, r, GB_DIM).astype(BF16)
        bi_scr[:, :, cols] = z[:, GB_DIM:].reshape(SWAP_T, r, GB_DIM).astype(BF16)
    for bl in range(SWAP_T):
        xb = jnp.concatenate([br_scr[bl], bi_scr[bl]], axis=0)
        y = jnp.dot(w_ref[bl], xb, preferred_element_type=F32)
        y_ref[0, bl] = y[:r].astype(BF16)
        y_ref[1, bl] = y[r:].astype(BF16)


def _chan_slow(u, m12):
    b = u.shape[0]
    r = DFT_RADIX
    return pl.pallas_call(
        _chan_slow_kernel,
        grid=(b, r // SWAP_T),
        in_specs=[pl.BlockSpec((None, r, SWAP_T, B_WIDTH), lambda bb, j: (bb, 0, j, 0)),
                  pl.BlockSpec((G_B, GB_DIM, 2 * GB_DIM), lambda bb, j: (0, 0, 0)),
                  pl.BlockSpec((SWAP_T, 2 * r, 2 * r), lambda bb, j: (j, 0, 0))],
        out_specs=pl.BlockSpec((None, 2, SWAP_T, r, B_WIDTH), lambda bb, j: (bb, 0, j, 0, 0)),
        out_shape=jax.ShapeDtypeStruct((b, 2, r, r, B_WIDTH), BF16),
        scratch_shapes=[pltpu.VMEM((SWAP_T, r, B_WIDTH), BF16)] * 2,
        compiler_params=_cparams(("arbitrary", "arbitrary")),
        name="chan_slow",
    )(u.reshape(b, r, r, B_WIDTH), m12, jnp.asarray(_slow_matrices(), BF16))


def _slow_matrices():
    n = DFT_RADIX * DFT_RADIX
    bb = np.arange(DFT_RADIX)[:, None, None]
    k1 = np.arange(DFT_RADIX)[None, :, None]
    a = np.arange(DFT_RADIX)[None, None, :]
    ang = 2.0 * np.pi * ((DFT_RADIX * a * k1 + bb * k1) % n) / n
    c, s = np.cos(ang), np.sin(ang)
    return np.concatenate([np.concatenate([c, s], axis=2), np.concatenate([-s, c], axis=2)], axis=1)


def _fast_matrix():
    c, s = _dft_cos_sin(DFT_RADIX)
    return np.concatenate([c, s], axis=1)


def _fast_kernel(w_ref, y_ref, g_ref, o_ref, o_scr, *, scale):
    zr = jnp.swapaxes(y_ref[0], 0, 1)
    zi = jnp.swapaxes(y_ref[1], 0, 1)
    w = w_ref[...]
    for kl in range(SWAP_T):
        z = jnp.concatenate([zr[kl], zi[kl]], axis=0)
        o_scr[kl] = jnp.dot(w, z, preferred_element_type=F32).astype(BF16)
    o = jnp.swapaxes(o_scr[...], 0, 1).astype(F32)
    o_ref[...] = ((o * scale) * g_ref[...].astype(F32)).astype(BF16)


def _dft_fast(y, p):
    b = y.shape[0]
    r = DFT_RADIX
    seq = r * r
    blk = pl.BlockSpec((None, r, SWAP_T, B_WIDTH), lambda bb, j: (bb, 0, j, 0))
    out = pl.pallas_call(
        functools.partial(_fast_kernel, scale=float((seq * GB_DIM) ** -0.5)),
        grid=(b, r // SWAP_T),
        in_specs=[
            pl.BlockSpec((r, 2 * r), lambda bb, j: (0, 0)),
            pl.BlockSpec((None, 2, r, SWAP_T, B_WIDTH), lambda bb, j: (bb, 0, 0, j, 0)),
            pl.BlockSpec((None, r, SWAP_T, B_WIDTH), lambda bb, j: (bb, 0, j, P_GB0 // B_WIDTH)),
        ],
        out_specs=blk,
        out_shape=jax.ShapeDtypeStruct((b, r, r, B_WIDTH), BF16),
        scratch_shapes=[pltpu.VMEM((SWAP_T, r, B_WIDTH), BF16)],
        compiler_params=_cparams(("arbitrary", "arbitrary")),
        name="dft_fast",
    )(jnp.asarray(_fast_matrix(), BF16), y, p.reshape(b, r, r, P_WIDTH))
    return out.reshape(b, seq, B_WIDTH)


def _ctx_dft_kernel(w_ref, xr_ref, xi_ref, g_ref, o_ref, *, scale):
    x = jnp.concatenate([xr_ref[...], xi_ref[...]], axis=0)
    o = jnp.dot(w_ref[...], x, preferred_element_type=F32)
    o_ref[...] = ((o * scale) * g_ref[...].astype(F32)).astype(BF16)


def _dft_ctx(br, bi, p):
    b, seq, _ = br.shape
    c, s = _dft_cos_sin(seq)
    w = jnp.asarray(np.concatenate([c, s], axis=1), BF16)
    blk = pl.BlockSpec((None, seq, B_WIDTH), lambda bb: (bb, 0, 0))
    return pl.pallas_call(
        functools.partial(_ctx_dft_kernel, scale=float((seq * GB_DIM) ** -0.5)),
        grid=(b,),
        in_specs=[pl.BlockSpec((seq, 2 * seq), lambda bb: (0, 0)), blk, blk,
                  pl.BlockSpec((None, seq, B_WIDTH), lambda bb: (bb, 0, P_GB0 // B_WIDTH))],
        out_specs=blk,
        out_shape=jax.ShapeDtypeStruct((b, seq, B_WIDTH), BF16),
        compiler_params=_cparams(("arbitrary",)),
        name="dft_ctx",
    )(w, br, bi, p)


OD_TM = 1024
OD_TN = 512
OD_RC = 256
OD_AHEAD = 4


def _inproj_odd_kernel(x_ref, sh_ref, sc_ref, nw_ref, wb_ref, wc_ref, wx_ref, wg_ref, cw_ref,
                       y_ref, h_scr, *, period, n_i):
    n = pl.program_id(2)
    tile = pl.program_id(0) * n_i + pl.program_id(1)
    slot = tile % 2
    tm = x_ref.shape[0]
    ahead_rows = tm // (C_WIDTH // OD_TN - OD_AHEAD)

    def norm_rows(rows):
        return _norm_mod(x_ref[rows, :], nw_ref[...], sh_ref[...], sc_ref[...]).astype(BF16)

    @pl.when(jnp.logical_and(tile == 0, n == 0))
    def _():
        h_scr[0] = norm_rows(slice(None))

    def compute(norm_ahead):
        cw = cw_ref[...]
        pos = lax.broadcasted_iota(jnp.int32, (OD_RC, 1), 0) % period
        for r in range(tm // OD_RC):
            rows = slice(r * OD_RC, (r + 1) * OD_RC)
            h = h_scr[slot, rows, :]
            z = (jnp.dot(h, wc_ref[...], preferred_element_type=F32)
                 * jnp.dot(h, wx_ref[...], preferred_element_type=F32))
            z_prev = jnp.where(pos == 0, 0.0, pltpu.roll(z, 1, 0))
            z_next = jnp.where(pos == period - 1, 0.0, pltpu.roll(z, OD_RC - 1, 0))
            conv = z_prev * cw[0:1, :] + z * cw[1:2, :] + z_next * cw[2:3, :]
            bg = jnp.dot(h, wb_ref[...], preferred_element_type=F32)
            g = jnp.dot(h, wg_ref[...], preferred_element_type=F32)
            y_ref[rows, :] = (bg * conv * _silu(g)).astype(BF16)
        if norm_ahead:
            rows = pl.ds(pl.multiple_of((n - OD_AHEAD) * ahead_rows, ahead_rows), ahead_rows)
            h_scr[1 - slot, rows, :] = norm_rows(rows)

    @pl.when(n < OD_AHEAD)
    def _():
        compute(False)

    @pl.when(n >= OD_AHEAD)
    def _():
        compute(True)


def _inproj_odd(x, shift, scale, nw, w, conv_w, layer, period):
    bx, lx, d = x.shape
    tm = min(OD_TM, lx)
    nblk = C_WIDTH // OD_TN
    n_i = lx // tm
    assert OD_RC % period == 0 and tm % OD_RC == 0
    x_map, mod_map = _ahead_maps(n_i, bx * n_i, OD_AHEAD)

    def w_spec(part):
        return pl.BlockSpec((None, d, OD_TN), lambda b, i, n: (layer, 0, part * nblk + n))

    return pl.pallas_call(
        functools.partial(_inproj_odd_kernel, period=period, n_i=n_i),
        grid=(bx, n_i, nblk),
        in_specs=[
            pl.BlockSpec((None, tm, d), x_map),
            pl.BlockSpec((None, 1, d), mod_map),
            pl.BlockSpec((None, 1, d), mod_map),
            pl.BlockSpec((1, d), lambda b, i, n: (0, 0)),
            w_spec(0), w_spec(1), w_spec(2), w_spec(3),
            pl.BlockSpec((None, 3, OD_TN), lambda b, i, n: (layer, 0, n)),
        ],
        out_specs=pl.BlockSpec((None, tm, OD_TN), lambda b, i, n: (b, i, n)),
        out_shape=jax.ShapeDtypeStruct((bx, lx, C_WIDTH), BF16),
        scratch_shapes=[pltpu.VMEM((2, tm, d), BF16)],
        compiler_params=_cparams(("arbitrary", "arbitrary", "arbitrary")),
        name="inproj_odd",
    )(x, shift, scale, nw, w, w, w, w, conv_w)


OUT_TM = 512


def _outproj_kernel(*refs, nparts, final):
    y_refs = refs[:nparts]
    w_refs = refs[nparts:2 * nparts]
    x_ref, g_ref, fw_ref, o_ref = refs[2 * nparts:]
    acc = jnp.dot(y_refs[0][...], w_refs[0][...], preferred_element_type=F32)
    for y_ref, w_ref in zip(y_refs[1:], w_refs[1:]):
        acc = acc + jnp.dot(y_ref[...], w_ref[...], preferred_element_type=F32)
    o = x_ref[...] + g_ref[...] * acc
    if final:
        o = (o * lax.rsqrt(jnp.mean(o * o, axis=-1, keepdims=True) + EPS)) * fw_ref[...]
    o_ref[...] = o


def _outproj(ys, w, layer, x, gate, fw, final):
    bx, lx, d = x.shape
    nparts = len(ys)
    kp = w.shape[1] // nparts
    tm = min(OUT_TM, lx)

    def w_spec(part):
        return pl.BlockSpec((None, kp, d), lambda b, i: (layer, part, 0),
                            pipeline_mode=pl.Buffered(1))

    row = pl.BlockSpec((None, tm, d), lambda b, i: (b, i, 0))
    return pl.pallas_call(
        functools.partial(_outproj_kernel, nparts=nparts, final=final),
        grid=(bx, lx // tm),
        in_specs=[pl.BlockSpec((None, tm, kp), lambda b, i: (b, i, 0)) for _ in ys]
        + [w_spec(part) for part in range(nparts)]
        + [row, pl.BlockSpec((None, 1, d), lambda b, i: (b, 0, 0)),
           pl.BlockSpec((1, d), lambda b, i: (0, 0))],
        out_specs=row,
        out_shape=jax.ShapeDtypeStruct((bx, lx, d), F32),
        compiler_params=_cparams(("arbitrary", "arbitrary")),
        name="outproj",
    )(*ys, *([w] * nparts), x, gate, fw)


def _rope_tables(seq):
    pos = jnp.arange(seq)
    row = (pos // GRID_W).astype(F32)
    col = (pos % GRID_W).astype(F32)
    nf = DK_A // 4
    inv = ROPE_BASE ** (-jnp.arange(nf, dtype=F32) / nf)
    ang = jnp.concatenate([row[:, None] * inv[None], col[:, None] * inv[None]], axis=-1)
    cos, sin = jnp.cos(ang), jnp.sin(ang)
    return jnp.concatenate([cos, cos], axis=-1), jnp.concatenate([-sin, sin], axis=-1)


def _even_layer(x, ctx, mod_x, mod_c, nw, w_in, layer, decay_logit, fno_w, w_out, tables, fw):
    b, seq, d = x.shape
    lc = ctx.shape[1]
    shift_x, scale_x, gate_x = mod_x
    shift_c, scale_c, gate_c = mod_c
    ctx_flat = ctx.reshape(1, b * lc, d)
    cosf, sinf, ones_t, zeros_t = tables
    log_gamma = -jnp.exp(decay_logit.astype(F32))

    px, ux = _inproj_even(x, shift_x, scale_x, nw, w_in, layer, cosf, sinf)
    pc, uc = _inproj_even(ctx_flat, shift_c, scale_c, nw, w_in, layer, ones_t, zeros_t)
    pc = pc.reshape(b, lc, P_WIDTH)
    uc = uc.reshape(b, lc, B_WIDTH)

    zeros_s = jnp.zeros((b, H_A, DK_A, DV_A), F32)
    ya_c, s_f, s_b = _ret(pc, zeros_s, zeros_s, log_gamma)
    ya_x, _, _ = _ret(px, s_f, s_b, log_gamma)

    m12 = _fold_fno(fno_w)
    yb_x = _dft_fast(_chan_slow(ux, m12), px)
    brc, bic = _chan_dft(uc, m12)
    yb_c = _dft_ctx(brc, bic, pc)

    x = _outproj([ya_x, yb_x], w_out, layer, x, gate_x, fw, False)
    ctx_flat = _outproj([ya_c.reshape(1, b * lc, A_WIDTH), yb_c.reshape(1, b * lc, B_WIDTH)],
                        w_out, layer, ctx_flat, gate_c, fw, False)
    return x, ctx_flat.reshape(b, lc, d)


def _odd_layer(x, ctx, mod_x, mod_c, nw, w_in, conv_w, w_out, layer, need_ctx, fw, final):
    b, seq, d = x.shape
    lc = ctx.shape[1]
    shift_x, scale_x, gate_x = mod_x
    y = _inproj_odd(x, shift_x, scale_x, nw, w_in, conv_w, layer, GRID_W)
    x = _outproj([y], w_out, layer, x, gate_x, fw, final)
    if need_ctx:
        shift_c, scale_c, gate_c = mod_c
        ctx_flat = ctx.reshape(1, b * lc, d)
        yc = _inproj_odd(ctx_flat, shift_c, scale_c, nw, w_in, conv_w, layer, lc)
        ctx = _outproj([yc], w_out, layer, ctx_flat, gate_c, fw, False).reshape(b, lc, d)
    return x, ctx


def kernel(x, c, ctx, c_ctx, ada_w, ada_b, norm_w, ev_w_in, ret_decay_logit, fno_w, ev_w_out,
           od_w_in, conv_w, od_w_out, final_norm_w):
    b, seq, d = x.shape
    lc = ctx.shape[1]
    depth = ada_w.shape[0]
    assert seq == DFT_RADIX * DFT_RADIX and seq % GRID_W == 0 and d == D_MODEL
    assert depth % 2 == 0

    rows = 16
    s = jnp.zeros((rows, d), F32).at[:b].set(c).at[b].set(c_ctx)
    mod = _ada_mod(s, ada_w, ada_b)

    ev_w_in = ev_w_in.astype(BF16)
    ev_w_out = ev_w_out.astype(BF16)
    od_w_in = od_w_in.astype(BF16)
    od_w_out = od_w_out.astype(BF16)

    fw = final_norm_w.reshape(1, d)
    cosf, sinf = _rope_tables(seq)
    tables = (cosf, sinf, jnp.ones((b * lc, DK_A), F32), jnp.zeros((b * lc, DK_A), F32))

    for i in range(depth):
        need_ctx = i < depth - 1
        mod_x = tuple(mod[i, :b, k * d:(k + 1) * d][:, None, :] for k in range(3))
        mod_c = tuple(mod[i, b:b + 1, k * d:(k + 1) * d][:, None, :] for k in range(3))
        nw = norm_w[i].reshape(1, d)
        j = i // 2
        if i % 2 == 0:
            x, ctx = _even_layer(x, ctx, mod_x, mod_c, nw, ev_w_in, j, ret_decay_logit[j],
                                 fno_w[j], ev_w_out, tables, fw)
        else:
            x, ctx = _odd_layer(x, ctx, mod_x, mod_c, nw, od_w_in, conv_w, od_w_out, j,
                                need_ctx, fw, i == depth - 1)
    return x
```

```python
import functools

import numpy as np
import jax
import jax.numpy as jnp
from jax import lax
from jax.experimental import pallas as pl
from jax.experimental.pallas import tpu as pltpu

F32 = jnp.float32
BF16 = jnp.bfloat16

D_MODEL = 2048
DEPTH = 4
GRID_W = 64
H_A = 8
DK_A = 128
DV_A = 256
QK_WIDTH = H_A * DK_A
A_WIDTH = H_A * DV_A
G_B = 8
GB_DIM = 256
B_WIDTH = G_B * GB_DIM
EVEN_IN = 2 * QK_WIDTH + 2 * A_WIDTH + 2 * B_WIDTH
C_WIDTH = 4096
ROPE_BASE = 10000.0
EPS = 1e-6
RET_CHUNK = 256
DFT_RADIX = 64

P_WIDTH = EVEN_IN
P_V0 = 2 * QK_WIDTH
P_GA0 = P_V0 + A_WIDTH
P_U0 = P_GA0 + A_WIDTH
P_GB0 = P_U0 + B_WIDTH

VMEM_LIMIT = 60 * 1024 * 1024


def _cparams(sem):
    return pltpu.CompilerParams(dimension_semantics=sem, vmem_limit_bytes=VMEM_LIMIT)


def _silu(x):
    hx = 0.5 * x
    return hx + hx * jnp.tanh(hx)


def _norm_mod(x, nw, sh, sc):
    y = x * lax.rsqrt(jnp.mean(x * x, axis=-1, keepdims=True) + EPS)
    return (y * nw) * (1.0 + sc) + sh


ADA_TN = 768


def _ada_kernel(s_ref, w_ref, b_ref, o_ref):
    s = _silu(s_ref[...]).astype(BF16)
    w = w_ref[...].astype(BF16)
    o_ref[...] = jnp.dot(s, w, preferred_element_type=F32) + b_ref[...]


def _ada_mod(s, ada_w, ada_b):
    depth, d, n3 = ada_w.shape
    rows = s.shape[0]
    return pl.pallas_call(
        _ada_kernel,
        grid=(depth, n3 // ADA_TN),
        in_specs=[
            pl.BlockSpec((rows, d), lambda l, n: (0, 0)),
            pl.BlockSpec((None, d, ADA_TN), lambda l, n: (l, 0, n)),
            pl.BlockSpec((None, 1, ADA_TN), lambda l, n: (l, 0, n)),
        ],
        out_specs=pl.BlockSpec((None, rows, ADA_TN), lambda l, n: (l, 0, n)),
        out_shape=jax.ShapeDtypeStruct((depth, rows, n3), F32),
        compiler_params=_cparams(("arbitrary", "arbitrary")),
        name="ada_mod",
    )(s, ada_w, ada_b.reshape(depth, 1, n3))


EV_TM = 1024
EV_TN = 2048
EV_CN = 512
EV_NT = EVEN_IN // EV_TN
EV_AHEAD = 1
EV_PLAIN_STEPS = (P_V0 // EV_TN, P_U0 // EV_TN)
EV_SILU_STEPS = (P_GA0 // EV_TN, P_GB0 // EV_TN)


def _ahead_maps(n_i, n_tiles, ahead_step):
    def tile(b, i, n):
        return jnp.minimum(b * n_i + i + jnp.where(n >= ahead_step, 1, 0), n_tiles - 1)

    def x_map(b, i, n):
        t = tile(b, i, n)
        return (t // n_i, t % n_i, 0)

    def mod_map(b, i, n):
        return (tile(b, i, n) // n_i, 0, 0)

    return x_map, mod_map


def _inproj_even_kernel(x_ref, sh_ref, sc_ref, nw_ref, w_ref, cos_ref, sin_ref,
                        p_ref, h_scr, *, kscale, n_i):
    n = pl.program_id(2)
    tile = pl.program_id(0) * n_i + pl.program_id(1)
    slot = tile % 2
    tm = x_ref.shape[0]

    def norm_rows(rows):
        return _norm_mod(x_ref[rows, :], nw_ref[...], sh_ref[...], sc_ref[...]).astype(BF16)

    @pl.when(jnp.logical_and(tile == 0, n == 0))
    def _():
        h_scr[0] = norm_rows(slice(None))

    def project(c):
        cols = slice(c * EV_CN, (c + 1) * EV_CN)
        return jnp.dot(h_scr[slot], w_ref[:, cols], preferred_element_type=F32)

    chunks = range(EV_TN // EV_CN)
    ahead_rows = tm // len(EV_PLAIN_STEPS)

    def norm_ahead():
        piece = jnp.where(n == EV_PLAIN_STEPS[0], 0, 1)
        rows = pl.ds(pl.multiple_of(piece * ahead_rows, ahead_rows), ahead_rows)
        h_scr[1 - slot, rows, :] = norm_rows(rows)

    @pl.when(n == 0)
    def _():
        cosf = cos_ref[...]
        sinf = sin_ref[...]
        for c in chunks:
            acc = project(c)
            for hh in range(EV_CN // DK_A):
                col0 = c * EV_CN + hh * DK_A
                t = acc[:, hh * DK_A:(hh + 1) * DK_A]
                o = t * cosf + pltpu.roll(t, DK_A // 2, 1) * sinf
                if col0 >= QK_WIDTH:
                    o = o * kscale
                p_ref[:, col0:col0 + DK_A] = o.astype(BF16)

    @pl.when(jnp.logical_or(n == EV_PLAIN_STEPS[0], n == EV_PLAIN_STEPS[1]))
    def _():
        for c in chunks:
            p_ref[:, c * EV_CN:(c + 1) * EV_CN] = project(c).astype(BF16)
        norm_ahead()

    @pl.when(jnp.logical_or(n == EV_SILU_STEPS[0], n == EV_SILU_STEPS[1]))
    def _():
        for c in chunks:
            p_ref[:, c * EV_CN:(c + 1) * EV_CN] = _silu(project(c)).astype(BF16)


def _inproj_even(x, shift, scale, nw, w, layer, cosf, sinf):
    bx, lx, d = x.shape
    tm = min(EV_TM, lx)
    n_i = lx // tm
    x_map, mod_map = _ahead_maps(n_i, bx * n_i, EV_AHEAD)
    return pl.pallas_call(
        functools.partial(_inproj_even_kernel, kscale=DK_A ** -0.5, n_i=n_i),
        grid=(bx, n_i, EV_NT),
        in_specs=[
            pl.BlockSpec((None, tm, d), x_map),
            pl.BlockSpec((None, 1, d), mod_map),
            pl.BlockSpec((None, 1, d), mod_map),
            pl.BlockSpec((1, d), lambda b, i, n: (0, 0)),
            pl.BlockSpec((None, d, EV_TN), lambda b, i, n: (layer, 0, n)),
            pl.BlockSpec((tm, DK_A), lambda b, i, n: (i, 0)),
            pl.BlockSpec((tm, DK_A), lambda b, i, n: (i, 0)),
        ],
        out_specs=pl.BlockSpec((None, tm, EV_TN), lambda b, i, n: (b, i, n)),
        out_shape=jax.ShapeDtypeStruct((bx, lx, P_WIDTH), BF16),
        scratch_shapes=[pltpu.VMEM((2, tm, d), BF16)],
        compiler_params=_cparams(("arbitrary", "arbitrary", "arbitrary")),
        name="inproj_even",
    )(x, shift, scale, nw, w, cosf, sinf)


def _ret_kernel(lg_ref, q_ref, k_ref, v_ref, g_ref, s0f_ref, s0b_ref,
                y_ref, sf_ref, sb_ref, o_scr, *, seq, chunk):
    h = pl.program_id(1)
    lgf = lg_ref[0, h]
    lgb = lg_ref[1, h]
    nc = seq // chunk
    cf = float(chunk)

    ii = lax.broadcasted_iota(jnp.int32, (chunk, chunk), 0)
    jj = lax.broadcasted_iota(jnp.int32, (chunk, chunk), 1)
    rel = (ii - jj).astype(F32)
    dmat = jnp.where(rel >= 0.0,
                     jnp.exp(lgf * jnp.maximum(rel, 0.0)),
                     jnp.exp(lgb * jnp.maximum(-rel, 0.0)))
    row = lax.broadcasted_iota(jnp.int32, (chunk, 1), 0).astype(F32)
    xi_f = jnp.exp(lgf * (row + 1.0))
    zeta_f = jnp.exp(lgf * (cf - 1.0 - row))
    xi_b = jnp.exp(lgb * (cf - row))
    zeta_b = jnp.exp(lgb * row)
    dec_f = jnp.exp(jnp.full((1, DV_A), lgf * cf, F32))
    dec_b = jnp.exp(jnp.full((1, DV_A), lgb * cf, F32))

    sf_ref[...] = s0f_ref[...]
    sb_ref[...] = s0b_ref[...]

    def rows(c):
        return pl.ds(pl.multiple_of(c * chunk, chunk), chunk)

    def forward(c):
        qn, kn, vn = q_ref[rows(c), :], k_ref[rows(c), :], v_ref[rows(c), :]
        s = sf_ref[...]
        sc = lax.dot_general(qn, kn, (((1,), (1,)), ((), ())), preferred_element_type=F32)
        o = jnp.dot((sc * dmat).astype(BF16), vn, preferred_element_type=F32)
        o = o + jnp.dot(qn, s.astype(BF16), preferred_element_type=F32) * xi_f
        kz = (kn.astype(F32) * zeta_f).astype(BF16)
        sf_ref[...] = dec_f * s + lax.dot_general(kz, vn, (((0,), (0,)), ((), ())),
                                                  preferred_element_type=F32)
        return o

    def backward(c):
        qn, kn, vn = q_ref[rows(c), :], k_ref[rows(c), :], v_ref[rows(c), :]
        s = sb_ref[...]
        o = jnp.dot(qn, s.astype(BF16), preferred_element_type=F32) * xi_b
        kz = (kn.astype(F32) * zeta_b).astype(BF16)
        sb_ref[...] = dec_b * s + lax.dot_general(kz, vn, (((0,), (0,)), ((), ())),
                                                  preferred_element_type=F32)
        return o

    def finish(c, o):
        o = o * lax.rsqrt(jnp.mean(o * o, axis=-1, keepdims=True) + EPS)
        y_ref[rows(c), :] = (o * g_ref[rows(c), :].astype(F32)).astype(BF16)

    if nc == 1:
        finish(0, forward(0) + backward(0))
    else:
        def first_half(t, carry):
            o_scr[rows(t), :] = forward(t)
            o_scr[rows(nc - 1 - t), :] = backward(nc - 1 - t)
            return carry

        def second_half(t, carry):
            finish(t, forward(t) + o_scr[rows(t), :])
            finish(nc - 1 - t, backward(nc - 1 - t) + o_scr[rows(nc - 1 - t), :])
            return carry

        lax.fori_loop(0, nc // 2, first_half, 0, unroll=True)
        lax.fori_loop(nc // 2, nc, second_half, 0, unroll=True)


def _ret(p, s0f, s0b, log_gamma):
    b, seq, _ = p.shape
    chunk = min(RET_CHUNK, seq)
    assert seq % chunk == 0 and (seq == chunk or (seq // chunk) % 2 == 0)
    st_spec = pl.BlockSpec((None, None, DK_A, DV_A), lambda bb, hh: (bb, hh, 0, 0))
    st_shape = jax.ShapeDtypeStruct((b, H_A, DK_A, DV_A), F32)
    return pl.pallas_call(
        functools.partial(_ret_kernel, seq=seq, chunk=chunk),
        grid=(b, H_A),
        in_specs=[
            pl.BlockSpec(memory_space=pltpu.SMEM),
            pl.BlockSpec((None, seq, DK_A), lambda bb, hh: (bb, 0, hh)),
            pl.BlockSpec((None, seq, DK_A), lambda bb, hh: (bb, 0, QK_WIDTH // DK_A + hh)),
            pl.BlockSpec((None, seq, DV_A), lambda bb, hh: (bb, 0, P_V0 // DV_A + hh)),
            pl.BlockSpec((None, seq, DV_A), lambda bb, hh: (bb, 0, P_GA0 // DV_A + hh)),
            st_spec, st_spec,
        ],
        out_specs=[
            pl.BlockSpec((None, seq, DV_A), lambda bb, hh: (bb, 0, hh)),
            st_spec, st_spec,
        ],
        out_shape=[jax.ShapeDtypeStruct((b, seq, A_WIDTH), BF16), st_shape, st_shape],
        scratch_shapes=[pltpu.VMEM((seq, DV_A), F32)],
        compiler_params=_cparams(("arbitrary", "arbitrary")),
        name="ret",
    )(log_gamma, p, p, p, p, s0f, s0b)


def _dft_cos_sin(n):
    kk = (np.arange(n)[:, None] * np.arange(n)[None, :]) % n
    ang = 2.0 * np.pi * kk / n
    return np.cos(ang), np.sin(ang)


def _fold_kernel(cc_ref, sc_ref, w_ref, o_ref):
    w = w_ref[...]
    o_ref[:, :GB_DIM] = jnp.dot(cc_ref[...], w, precision=lax.Precision.HIGHEST,
                                preferred_element_type=F32).astype(BF16)
    o_ref[:, GB_DIM:] = (-jnp.dot(sc_ref[...], w, precision=lax.Precision.HIGHEST,
                                  preferred_element_type=F32)).astype(BF16)


def _fold_fno(fno_w):
    cc, sc = _dft_cos_sin(GB_DIM)
    mat = pl.BlockSpec((GB_DIM, GB_DIM), lambda g: (0, 0))
    return pl.pallas_call(
        _fold_kernel,
        grid=(G_B,),
        in_specs=[mat, mat, pl.BlockSpec((None, GB_DIM, GB_DIM), lambda g: (g, 0, 0))],
        out_specs=pl.BlockSpec((None, GB_DIM, 2 * GB_DIM), lambda g: (g, 0, 0)),
        out_shape=jax.ShapeDtypeStruct((G_B, GB_DIM, 2 * GB_DIM), BF16),
        compiler_params=_cparams(("arbitrary",)),
        name="fold_fno",
    )(jnp.asarray(cc, F32), jnp.asarray(sc, F32), fno_w)


CH_TM = 2048


def _chan_kernel(u_ref, m_ref, br_ref, bi_ref):
    r = jnp.dot(u_ref[...], m_ref[...], preferred_element_type=F32)
    br_ref[...] = r[:, :GB_DIM].astype(BF16)
    bi_ref[...] = r[:, GB_DIM:].astype(BF16)


def _chan_dft(p, m12):
    bx, lx, _ = p.shape
    tm = min(CH_TM, lx)
    col = pl.BlockSpec((None, tm, GB_DIM), lambda b, i, g: (b, i, g))
    shp = jax.ShapeDtypeStruct((bx, lx, B_WIDTH), BF16)
    return pl.pallas_call(
        _chan_kernel,
        grid=(bx, lx // tm, G_B),
        in_specs=[pl.BlockSpec((None, tm, GB_DIM), lambda b, i, g: (b, i, P_U0 // GB_DIM + g)),
                  pl.BlockSpec((None, GB_DIM, 2 * GB_DIM), lambda b, i, g: (g, 0, 0))],
        out_specs=[col, col],
        out_shape=[shp, shp],
        compiler_params=_cparams(("arbitrary", "arbitrary", "arbitrary")),
        name="chan_dft",
    )(p, m12)


SWAP_T = 16


def _chan_slow_kernel(u_ref, m_ref, w_ref, y_ref, br_scr, bi_scr):
    r = DFT_RADIX
    x = jnp.swapaxes(u_ref[...], 0, 1).reshape(SWAP_T * r, B_WIDTH)
    for g in range(G_B):
        cols = slice(g * GB_DIM, (g + 1) * GB_DIM)
        z = jnp.dot(x[:, cols], m_ref[g], preferred_element_type=F32)
        br_scr[:, :, cols] = z[:, :GB_DIM].reshape(SWAP_T, r, GB_DIM).astype(BF16)
        bi_scr[:, :, cols] = z[:, GB_DIM:].reshape(SWAP_T, r, GB_DIM).astype(BF16)
    for bl in range(SWAP_T):
        xb = jnp.concatenate([br_scr[bl], bi_scr[bl]], axis=0)
        y = jnp.dot(w_ref[bl], xb, preferred_element_type=F32)
        y_ref[0, bl] = y[:r].astype(BF16)
        y_ref[1, bl] = y[r:].astype(BF16)


def _chan_slow(p, m12):
    b = p.shape[0]
    r = DFT_RADIX
    return pl.pallas_call(
        _chan_slow_kernel,
        grid=(b, r // SWAP_T),
        in_specs=[pl.BlockSpec((None, r, SWAP_T, B_WIDTH),
                               lambda bb, j: (bb, 0, j, P_U0 // B_WIDTH)),
                  pl.BlockSpec((G_B, GB_DIM, 2 * GB_DIM), lambda bb, j: (0, 0, 0)),
                  pl.BlockSpec((SWAP_T, 2 * r, 2 * r), lambda bb, j: (j, 0, 0))],
        out_specs=pl.BlockSpec((None, 2, SWAP_T, r, B_WIDTH), lambda bb, j: (bb, 0, j, 0, 0)),
        out_shape=jax.ShapeDtypeStruct((b, 2, r, r, B_WIDTH), BF16),
        scratch_shapes=[pltpu.VMEM((SWAP_T, r, B_WIDTH), BF16)] * 2,
        compiler_params=_cparams(("arbitrary", "arbitrary")),
        name="chan_slow",
    )(p.reshape(b, r, r, P_WIDTH), m12, jnp.asarray(_slow_matrices(), BF16))


def _slow_matrices():
    n = DFT_RADIX * DFT_RADIX
    bb = np.arange(DFT_RADIX)[:, None, None]
    k1 = np.arange(DFT_RADIX)[None, :, None]
    a = np.arange(DFT_RADIX)[None, None, :]
    ang = 2.0 * np.pi * ((DFT_RADIX * a * k1 + bb * k1) % n) / n
    c, s = np.cos(ang), np.sin(ang)
    return np.concatenate([np.concatenate([c, s], axis=2), np.concatenate([-s, c], axis=2)], axis=1)


def _fast_matrix():
    c, s = _dft_cos_sin(DFT_RADIX)
    return np.concatenate([c, s], axis=1)


def _fast_kernel(w_ref, y_ref, g_ref, o_ref, o_scr, *, scale):
    zr = jnp.swapaxes(y_ref[0], 0, 1)
    zi = jnp.swapaxes(y_ref[1], 0, 1)
    w = w_ref[...]
    for kl in range(SWAP_T):
        z = jnp.concatenate([zr[kl], zi[kl]], axis=0)
        o_scr[kl] = jnp.dot(w, z, preferred_element_type=F32).astype(BF16)
    o = jnp.swapaxes(o_scr[...], 0, 1).astype(F32)
    o_ref[...] = ((o * scale) * g_ref[...].astype(F32)).astype(BF16)


def _dft_fast(y, p):
    b = y.shape[0]
    r = DFT_RADIX
    seq = r * r
    blk = pl.BlockSpec((None, r, SWAP_T, B_WIDTH), lambda bb, j: (bb, 0, j, 0))
    out = pl.pallas_call(
        functools.partial(_fast_kernel, scale=float((seq * GB_DIM) ** -0.5)),
        grid=(b, r // SWAP_T),
        in_specs=[
            pl.BlockSpec((r, 2 * r), lambda bb, j: (0, 0)),
            pl.BlockSpec((None, 2, r, SWAP_T, B_WIDTH), lambda bb, j: (bb, 0, 0, j, 0)),
            pl.BlockSpec((None, r, SWAP_T, B_WIDTH), lambda bb, j: (bb, 0, j, P_GB0 // B_WIDTH)),
        ],
        out_specs=blk,
        out_shape=jax.ShapeDtypeStruct((b, r, r, B_WIDTH), BF16),
        scratch_shapes=[pltpu.VMEM((SWAP_T, r, B_WIDTH), BF16)],
        compiler_params=_cparams(("arbitrary", "arbitrary")),
        name="dft_fast",
    )(jnp.asarray(_fast_matrix(), BF16), y, p.reshape(b, r, r, P_WIDTH))
    return out.reshape(b, seq, B_WIDTH)


def _ctx_dft_kernel(w_ref, xr_ref, xi_ref, g_ref, o_ref, *, scale):
    x = jnp.concatenate([xr_ref[...], xi_ref[...]], axis=0)
    o = jnp.dot(w_ref[...], x, preferred_element_type=F32)
    o_ref[...] = ((o * scale) * g_ref[...].astype(F32)).astype(BF16)


def _dft_ctx(br, bi, p):
    b, seq, _ = br.shape
    c, s = _dft_cos_sin(seq)
    w = jnp.asarray(np.concatenate([c, s], axis=1), BF16)
    blk = pl.BlockSpec((None, seq, B_WIDTH), lambda bb: (bb, 0, 0))
    return pl.pallas_call(
        functools.partial(_ctx_dft_kernel, scale=float((seq * GB_DIM) ** -0.5)),
        grid=(b,),
        in_specs=[pl.BlockSpec((seq, 2 * seq), lambda bb: (0, 0)), blk, blk,
                  pl.BlockSpec((None, seq, B_WIDTH), lambda bb: (bb, 0, P_GB0 // B_WIDTH))],
        out_specs=blk,
        out_shape=jax.ShapeDtypeStruct((b, seq, B_WIDTH), BF16),
        compiler_params=_cparams(("arbitrary",)),
        name="dft_ctx",
    )(w, br, bi, p)


OD_TM = 1024
OD_TN = 512
OD_RC = 256
OD_AHEAD = 4


def _inproj_odd_kernel(x_ref, sh_ref, sc_ref, nw_ref, wb_ref, wc_ref, wx_ref, wg_ref, cw_ref,
                       y_ref, h_scr, *, period, n_i):
    n = pl.program_id(2)
    tile = pl.program_id(0) * n_i + pl.program_id(1)
    slot = tile % 2
    tm = x_ref.shape[0]
    ahead_rows = tm // ((C_WIDTH // OD_TN - OD_AHEAD) * (tm // OD_RC))

    def norm_rows(rows):
        return _norm_mod(x_ref[rows, :], nw_ref[...], sh_ref[...], sc_ref[...]).astype(BF16)

    @pl.when(jnp.logical_and(tile == 0, n == 0))
    def _():
        h_scr[0] = norm_rows(slice(None))

    def compute(norm_ahead):
        cw = cw_ref[...]
        pos = lax.broadcasted_iota(jnp.int32, (OD_RC, 1), 0) % period
        for r in range(tm // OD_RC):
            rows = slice(r * OD_RC, (r + 1) * OD_RC)
            h = h_scr[slot, rows, :]
            z = (jnp.dot(h, wc_ref[...], preferred_element_type=F32)
                 * jnp.dot(h, wx_ref[...], preferred_element_type=F32))
            z_prev = jnp.where(pos == 0, 0.0, pltpu.roll(z, 1, 0))
            z_next = jnp.where(pos == period - 1, 0.0, pltpu.roll(z, OD_RC - 1, 0))
            conv = z_prev * cw[0:1, :] + z * cw[1:2, :] + z_next * cw[2:3, :]
            bg = jnp.dot(h, wb_ref[...], preferred_element_type=F32)
            g = jnp.dot(h, wg_ref[...], preferred_element_type=F32)
            y_ref[rows, :] = (bg * conv * _silu(g)).astype(BF16)
            if norm_ahead:
                piece = (n - OD_AHEAD) * (tm // OD_RC) + r
                nrows = pl.ds(pl.multiple_of(piece * ahead_rows, ahead_rows), ahead_rows)
                h_scr[1 - slot, nrows, :] = norm_rows(nrows)

    @pl.when(n < OD_AHEAD)
    def _():
        compute(False)

    @pl.when(n >= OD_AHEAD)
    def _():
        compute(True)


def _inproj_odd(x, shift, scale, nw, w, conv_w, layer, period):
    bx, lx, d = x.shape
    tm = min(OD_TM, lx)
    nblk = C_WIDTH // OD_TN
    n_i = lx // tm
    assert OD_RC % period == 0 and tm % OD_RC == 0
    x_map, mod_map = _ahead_maps(n_i, bx * n_i, OD_AHEAD)

    def w_spec(part):
        return pl.BlockSpec((None, d, OD_TN), lambda b, i, n: (layer, 0, part * nblk + n))

    return pl.pallas_call(
        functools.partial(_inproj_odd_kernel, period=period, n_i=n_i),
        grid=(bx, n_i, nblk),
        in_specs=[
            pl.BlockSpec((None, tm, d), x_map),
            pl.BlockSpec((None, 1, d), mod_map),
            pl.BlockSpec((None, 1, d), mod_map),
            pl.BlockSpec((1, d), lambda b, i, n: (0, 0)),
            w_spec(0), w_spec(1), w_spec(2), w_spec(3),
            pl.BlockSpec((None, 3, OD_TN), lambda b, i, n: (layer, 0, n)),
        ],
        out_specs=pl.BlockSpec((None, tm, OD_TN), lambda b, i, n: (b, i, n)),
        out_shape=jax.ShapeDtypeStruct((bx, lx, C_WIDTH), BF16),
        scratch_shapes=[pltpu.VMEM((2, tm, d), BF16)],
        compiler_params=_cparams(("arbitrary", "arbitrary", "arbitrary")),
        name="inproj_odd",
    )(x, shift, scale, nw, w, w, w, w, conv_w)


OUT_TM = 512


def _outproj_kernel(*refs, nparts, final):
    y_refs = refs[:nparts]
    w_refs = refs[nparts:2 * nparts]
    x_ref, g_ref, fw_ref, o_ref = refs[2 * nparts:]
    acc = jnp.dot(y_refs[0][...], w_refs[0][...], preferred_element_type=F32)
    for y_ref, w_ref in zip(y_refs[1:], w_refs[1:]):
        acc = acc + jnp.dot(y_ref[...], w_ref[...], preferred_element_type=F32)
    o = x_ref[...] + g_ref[...] * acc
    if final:
        o = (o * lax.rsqrt(jnp.mean(o * o, axis=-1, keepdims=True) + EPS)) * fw_ref[...]
    o_ref[...] = o


def _outproj(ys, w, layer, x, gate, fw, final):
    bx, lx, d = x.shape
    nparts = len(ys)
    kp = w.shape[1] // nparts
    tm = min(OUT_TM, lx)

    def w_spec(part):
        return pl.BlockSpec((None, kp, d), lambda b, i: (layer, part, 0),
                            pipeline_mode=pl.Buffered(1))

    row = pl.BlockSpec((None, tm, d), lambda b, i: (b, i, 0))
    return pl.pallas_call(
        functools.partial(_outproj_kernel, nparts=nparts, final=final),
        grid=(bx, lx // tm),
        in_specs=[pl.BlockSpec((None, tm, kp), lambda b, i: (b, i, 0)) for _ in ys]
        + [w_spec(part) for part in range(nparts)]
        + [row, pl.BlockSpec((None, 1, d), lambda b, i: (b, 0, 0)),
           pl.BlockSpec((1, d), lambda b, i: (0, 0))],
        out_specs=row,
        out_shape=jax.ShapeDtypeStruct((bx, lx, d), F32),
        compiler_params=_cparams(("arbitrary", "arbitrary")),
        name="outproj",
    )(*ys, *([w] * nparts), x, gate, fw)


def _rope_tables(seq):
    pos = jnp.arange(seq)
    row = (pos // GRID_W).astype(F32)
    col = (pos % GRID_W).astype(F32)
    nf = DK_A // 4
    inv = ROPE_BASE ** (-jnp.arange(nf, dtype=F32) / nf)
    ang = jnp.concatenate([row[:, None] * inv[None], col[:, None] * inv[None]], axis=-1)
    cos, sin = jnp.cos(ang), jnp.sin(ang)
    return jnp.concatenate([cos, cos], axis=-1), jnp.concatenate([-sin, sin], axis=-1)


def _even_layer(x, ctx, mod_x, mod_c, nw, w_in, layer, decay_logit, fno_w, w_out, tables, fw):
    b, seq, d = x.shape
    lc = ctx.shape[1]
    shift_x, scale_x, gate_x = mod_x
    shift_c, scale_c, gate_c = mod_c
    ctx_flat = ctx.reshape(1, b * lc, d)
    cosf, sinf, ones_t, zeros_t = tables
    log_gamma = -jnp.exp(decay_logit.astype(F32))

    px = _inproj_even(x, shift_x, scale_x, nw, w_in, layer, cosf, sinf)
    pc = _inproj_even(ctx_flat, shift_c, scale_c, nw, w_in, layer, ones_t, zeros_t)
    pc = pc.reshape(b, lc, P_WIDTH)

    zeros_s = jnp.zeros((b, H_A, DK_A, DV_A), F32)
    ya_c, s_f, s_b = _ret(pc, zeros_s, zeros_s, log_gamma)
    ya_x, _, _ = _ret(px, s_f, s_b, log_gamma)

    m12 = _fold_fno(fno_w)
    yb_x = _dft_fast(_chan_slow(px, m12), px)
    brc, bic = _chan_dft(pc, m12)
    yb_c = _dft_ctx(brc, bic, pc)

    x = _outproj([ya_x, yb_x], w_out, layer, x, gate_x, fw, False)
    ctx_flat = _outproj([ya_c.reshape(1, b * lc, A_WIDTH), yb_c.reshape(1, b * lc, B_WIDTH)],
                        w_out, layer, ctx_flat, gate_c, fw, False)
    return x, ctx_flat.reshape(b, lc, d)


def _odd_layer(x, ctx, mod_x, mod_c, nw, w_in, conv_w, w_out, layer, need_ctx, fw, final):
    b, seq, d = x.shape
    lc = ctx.shape[1]
    shift_x, scale_x, gate_x = mod_x
    y = _inproj_odd(x, shift_x, scale_x, nw, w_in, conv_w, layer, GRID_W)
    x = _outproj([y], w_out, layer, x, gate_x, fw, final)
    if need_ctx:
        shift_c, scale_c, gate_c = mod_c
        ctx_flat = ctx.reshape(1, b * lc, d)
        yc = _inproj_odd(ctx_flat, shift_c, scale_c, nw, w_in, conv_w, layer, lc)
        ctx = _outproj([yc], w_out, layer, ctx_flat, gate_c, fw, False).reshape(b, lc, d)
    return x, ctx


def kernel(x, c, ctx, c_ctx, ada_w, ada_b, norm_w, ev_w_in, ret_decay_logit, fno_w, ev_w_out,
           od_w_in, conv_w, od_w_out, final_norm_w):
    b, seq, d = x.shape
    lc = ctx.shape[1]
    depth = ada_w.shape[0]
    assert seq == DFT_RADIX * DFT_RADIX and seq % GRID_W == 0 and d == D_MODEL
    assert depth % 2 == 0

    rows = 16
    s = jnp.zeros((rows, d), F32).at[:b].set(c).at[b].set(c_ctx)
    mod = _ada_mod(s, ada_w, ada_b)

    ev_w_in = ev_w_in.astype(BF16)
    ev_w_out = ev_w_out.astype(BF16)
    od_w_in = od_w_in.astype(BF16)
    od_w_out = od_w_out.astype(BF16)

    fw = final_norm_w.reshape(1, d)
    cosf, sinf = _rope_tables(seq)
    tables = (cosf, sinf, jnp.ones((b * lc, DK_A), F32), jnp.zeros((b * lc, DK_A), F32))

    for i in range(depth):
        need_ctx = i < depth - 1
        mod_x = tuple(mod[i, :b, k * d:(k + 1) * d][:, None, :] for k in range(3))
        mod_c = tuple(mod[i, b:b + 1, k * d:(k + 1) * d][:, None, :] for k in range(3))
        nw = norm_w[i].reshape(1, d)
        j = i // 2
        if i % 2 == 0:
            x, ctx = _even_layer(x, ctx, mod_x, mod_c, nw, ev_w_in, j, ret_decay_logit[j],
                                 fno_w[j], ev_w_out, tables, fw)
        else:
            x, ctx = _odd_layer(x, ctx, mod_x, mod_c, nw, od_w_in, conv_w, od_w_out, j,
                                need_ctx, fw, i == depth - 1)
    return x
```

```python
import functools

import numpy as np
import jax
import jax.numpy as jnp
from jax import lax
from jax.experimental import pallas as pl
from jax.experimental.pallas import tpu as pltpu

F32 = jnp.float32
BF16 = jnp.bfloat16

D_MODEL = 2048
DEPTH = 4
GRID_W = 64
H_A = 8
DK_A = 128
DV_A = 256
QK_WIDTH = H_A * DK_A
A_WIDTH = H_A * DV_A
G_B = 8
GB_DIM = 256
B_WIDTH = G_B * GB_DIM
EVEN_IN = 2 * QK_WIDTH + 2 * A_WIDTH + 2 * B_WIDTH
C_WIDTH = 4096
ROPE_BASE = 10000.0
EPS = 1e-6
RET_CHUNK = 256
RET_HEADS = 2
DFT_RADIX = 64

P_WIDTH = EVEN_IN
P_V0 = 2 * QK_WIDTH
P_GA0 = P_V0 + A_WIDTH
P_U0 = P_GA0 + A_WIDTH
P_GB0 = P_U0 + B_WIDTH

VMEM_LIMIT = 60 * 1024 * 1024


def _cparams(sem):
    return pltpu.CompilerParams(dimension_semantics=sem, vmem_limit_bytes=VMEM_LIMIT)


def _silu(x):
    hx = 0.5 * x
    return hx + hx * jnp.tanh(hx)


def _norm_mod(x, nw, sh, sc):
    y = x * lax.rsqrt(jnp.mean(x * x, axis=-1, keepdims=True) + EPS)
    return (y * nw) * (1.0 + sc) + sh


ADA_TN = 768


def _ada_kernel(s_ref, w_ref, b_ref, o_ref):
    s = _silu(s_ref[...]).astype(BF16)
    w = w_ref[...].astype(BF16)
    o_ref[...] = jnp.dot(s, w, preferred_element_type=F32) + b_ref[...]


def _ada_mod(s, ada_w, ada_b):
    depth, d, n3 = ada_w.shape
    rows = s.shape[0]
    return pl.pallas_call(
        _ada_kernel,
        grid=(depth, n3 // ADA_TN),
        in_specs=[
            pl.BlockSpec((rows, d), lambda l, n: (0, 0)),
            pl.BlockSpec((None, d, ADA_TN), lambda l, n: (l, 0, n)),
            pl.BlockSpec((None, 1, ADA_TN), lambda l, n: (l, 0, n)),
        ],
        out_specs=pl.BlockSpec((None, rows, ADA_TN), lambda l, n: (l, 0, n)),
        out_shape=jax.ShapeDtypeStruct((depth, rows, n3), F32),
        compiler_params=_cparams(("arbitrary", "arbitrary")),
        name="ada_mod",
    )(s, ada_w, ada_b.reshape(depth, 1, n3))


EV_TM = 1024
EV_TN = 2048
EV_CN = 512
EV_NT = EVEN_IN // EV_TN
EV_AHEAD = 1
EV_PLAIN_STEPS = (P_V0 // EV_TN, P_U0 // EV_TN)
EV_SILU_STEPS = (P_GA0 // EV_TN, P_GB0 // EV_TN)


def _ahead_maps(n_i, n_tiles, ahead_step):
    def tile(b, i, n):
        return jnp.minimum(b * n_i + i + jnp.where(n >= ahead_step, 1, 0), n_tiles - 1)

    def x_map(b, i, n):
        t = tile(b, i, n)
        return (t // n_i, t % n_i, 0)

    def mod_map(b, i, n):
        return (tile(b, i, n) // n_i, 0, 0)

    return x_map, mod_map


def _inproj_even_kernel(x_ref, sh_ref, sc_ref, nw_ref, w_ref, cos_ref, sin_ref,
                        p_ref, h_scr, *, kscale, n_i):
    n = pl.program_id(2)
    tile = pl.program_id(0) * n_i + pl.program_id(1)
    slot = tile % 2
    tm = x_ref.shape[0]

    def norm_rows(rows):
        return _norm_mod(x_ref[rows, :], nw_ref[...], sh_ref[...], sc_ref[...]).astype(BF16)

    @pl.when(jnp.logical_and(tile == 0, n == 0))
    def _():
        h_scr[0] = norm_rows(slice(None))

    def project(c):
        cols = slice(c * EV_CN, (c + 1) * EV_CN)
        return jnp.dot(h_scr[slot], w_ref[:, cols], preferred_element_type=F32)

    chunks = range(EV_TN // EV_CN)
    ahead_rows = tm // len(EV_PLAIN_STEPS)

    def norm_ahead():
        piece = jnp.where(n == EV_PLAIN_STEPS[0], 0, 1)
        rows = pl.ds(pl.multiple_of(piece * ahead_rows, ahead_rows), ahead_rows)
        h_scr[1 - slot, rows, :] = norm_rows(rows)

    @pl.when(n == 0)
    def _():
        cosf = cos_ref[...]
        sinf = sin_ref[...]
        for c in chunks:
            acc = project(c)
            for hh in range(EV_CN // DK_A):
                col0 = c * EV_CN + hh * DK_A
                t = acc[:, hh * DK_A:(hh + 1) * DK_A]
                o = t * cosf + pltpu.roll(t, DK_A // 2, 1) * sinf
                if col0 >= QK_WIDTH:
                    o = o * kscale
                p_ref[:, col0:col0 + DK_A] = o.astype(BF16)

    @pl.when(jnp.logical_or(n == EV_PLAIN_STEPS[0], n == EV_PLAIN_STEPS[1]))
    def _():
        for c in chunks:
            p_ref[:, c * EV_CN:(c + 1) * EV_CN] = project(c).astype(BF16)
        norm_ahead()

    @pl.when(jnp.logical_or(n == EV_SILU_STEPS[0], n == EV_SILU_STEPS[1]))
    def _():
        for c in chunks:
            p_ref[:, c * EV_CN:(c + 1) * EV_CN] = _silu(project(c)).astype(BF16)


def _inproj_even(x, shift, scale, nw, w, layer, cosf, sinf):
    bx, lx, d = x.shape
    tm = min(EV_TM, lx)
    n_i = lx // tm
    x_map, mod_map = _ahead_maps(n_i, bx * n_i, EV_AHEAD)
    return pl.pallas_call(
        functools.partial(_inproj_even_kernel, kscale=DK_A ** -0.5, n_i=n_i),
        grid=(bx, n_i, EV_NT),
        in_specs=[
            pl.BlockSpec((None, tm, d), x_map),
            pl.BlockSpec((None, 1, d), mod_map),
            pl.BlockSpec((None, 1, d), mod_map),
            pl.BlockSpec((1, d), lambda b, i, n: (0, 0)),
            pl.BlockSpec((None, d, EV_TN), lambda b, i, n: (layer, 0, n)),
            pl.BlockSpec((tm, DK_A), lambda b, i, n: (i, 0)),
            pl.BlockSpec((tm, DK_A), lambda b, i, n: (i, 0)),
        ],
        out_specs=pl.BlockSpec((None, tm, EV_TN), lambda b, i, n: (b, i, n)),
        out_shape=jax.ShapeDtypeStruct((bx, lx, P_WIDTH), BF16),
        scratch_shapes=[pltpu.VMEM((2, tm, d), BF16)],
        compiler_params=_cparams(("arbitrary", "arbitrary", "arbitrary")),
        name="inproj_even",
    )(x, shift, scale, nw, w, cosf, sinf)


def _ret_kernel(lg_ref, q_ref, k_ref, v_ref, g_ref, s0f_ref, s0b_ref,
                y_ref, sf_ref, sb_ref, st_scr, *, seq, chunk):
    nc = seq // chunk
    cf = float(chunk)

    def rows(c):
        return pl.ds(pl.multiple_of(c * chunk, chunk), chunk)

    sf_ref[...] = s0f_ref[...]
    sb_ref[...] = s0b_ref[...]

    def head_ops(hh):
        h = pl.program_id(1) * RET_HEADS + hh
        lgf = lg_ref[0, h]
        lgb = lg_ref[1, h]
        qk = slice(hh * DK_A, (hh + 1) * DK_A)
        vv = slice(hh * DV_A, (hh + 1) * DV_A)
        ii = lax.broadcasted_iota(jnp.int32, (chunk, chunk), 0)
        jj = lax.broadcasted_iota(jnp.int32, (chunk, chunk), 1)
        rel = (ii - jj).astype(F32)
        dmat = jnp.where(rel >= 0.0,
                         jnp.exp(lgf * jnp.maximum(rel, 0.0)),
                         jnp.exp(lgb * jnp.maximum(-rel, 0.0)))
        row = lax.broadcasted_iota(jnp.int32, (chunk, DK_A), 0).astype(F32)
        xi_f = jnp.exp(lgf * (row + 1.0)).astype(BF16)
        zeta_f = jnp.exp(lgf * (cf - 1.0 - row)).astype(BF16)
        xi_b = jnp.exp(lgb * (cf - row)).astype(BF16)
        zeta_b = jnp.exp(lgb * row).astype(BF16)
        dec_f = jnp.exp(jnp.full((1, DV_A), lgf * cf, F32))
        dec_b = jnp.exp(jnp.full((1, DV_A), lgb * cf, F32))

        def record_and_advance(s_ref, half, c, zeta, dec):
            s = s_ref[hh]
            st_scr[hh, c, half * DK_A:(half + 1) * DK_A, :] = s.astype(BF16)
            kz = k_ref[rows(c), qk] * zeta
            s_ref[hh] = dec * s + lax.dot_general(kz, v_ref[rows(c), vv], (((0,), (0,)), ((), ())),
                                                  preferred_element_type=F32)

        def states(t):
            record_and_advance(sf_ref, 0, t, zeta_f, dec_f)
            record_and_advance(sb_ref, 1, nc - 1 - t, zeta_b, dec_b)

        def output(c):
            qn, kn, vn = q_ref[rows(c), qk], k_ref[rows(c), qk], v_ref[rows(c), vv]
            sc = lax.dot_general(qn, kn, (((1,), (1,)), ((), ())), preferred_element_type=F32)
            o = jnp.dot((sc * dmat).astype(BF16), vn, preferred_element_type=F32)
            q2 = jnp.concatenate([qn * xi_f, qn * xi_b], axis=1)
            o = o + jnp.dot(q2, st_scr[hh, c], preferred_element_type=F32)
            o = o * lax.rsqrt(jnp.mean(o * o, axis=-1, keepdims=True) + EPS)
            y_ref[rows(c), vv] = (o * g_ref[rows(c), vv].astype(F32)).astype(BF16)

        return states, output

    heads = [head_ops(hh) for hh in range(RET_HEADS)]

    def state_sweep(t, carry):
        for states, _ in heads:
            states(t)
        return carry

    def output_sweep(c, carry):
        for _, output in heads:
            output(c)
        return carry

    lax.fori_loop(0, nc, state_sweep, 0, unroll=True)
    lax.fori_loop(0, nc, output_sweep, 0, unroll=True)


def _ret(p, s0f, s0b, log_gamma):
    b, seq, _ = p.shape
    chunk = min(RET_CHUNK, seq)
    assert seq % chunk == 0
    nh = RET_HEADS
    st_spec = pl.BlockSpec((None, nh, DK_A, DV_A), lambda bb, hp: (bb, hp, 0, 0))
    st_shape = jax.ShapeDtypeStruct((b, H_A, DK_A, DV_A), F32)
    return pl.pallas_call(
        functools.partial(_ret_kernel, seq=seq, chunk=chunk),
        grid=(b, H_A // nh),
        in_specs=[
            pl.BlockSpec(memory_space=pltpu.SMEM),
            pl.BlockSpec((None, seq, nh * DK_A), lambda bb, hp: (bb, 0, hp)),
            pl.BlockSpec((None, seq, nh * DK_A), lambda bb, hp: (bb, 0, H_A // nh + hp)),
            pl.BlockSpec((None, seq, nh * DV_A), lambda bb, hp: (bb, 0, P_V0 // (nh * DV_A) + hp)),
            pl.BlockSpec((None, seq, nh * DV_A), lambda bb, hp: (bb, 0, P_GA0 // (nh * DV_A) + hp)),
            st_spec, st_spec,
        ],
        out_specs=[
            pl.BlockSpec((None, seq, nh * DV_A), lambda bb, hp: (bb, 0, hp)),
            st_spec, st_spec,
        ],
        out_shape=[jax.ShapeDtypeStruct((b, seq, A_WIDTH), BF16), st_shape, st_shape],
        scratch_shapes=[pltpu.VMEM((nh, seq // chunk, 2 * DK_A, DV_A), BF16)],
        compiler_params=_cparams(("arbitrary", "arbitrary")),
        name="ret",
    )(log_gamma, p, p, p, p, s0f, s0b)


def _dft_cos_sin(n):
    kk = (np.arange(n)[:, None] * np.arange(n)[None, :]) % n
    ang = 2.0 * np.pi * kk / n
    return np.cos(ang), np.sin(ang)


def _fold_kernel(cc_ref, sc_ref, w_ref, o_ref):
    w = w_ref[...]
    o_ref[:, :GB_DIM] = jnp.dot(cc_ref[...], w, precision=lax.Precision.HIGHEST,
                                preferred_element_type=F32).astype(BF16)
    o_ref[:, GB_DIM:] = (-jnp.dot(sc_ref[...], w, precision=lax.Precision.HIGHEST,
                                  preferred_element_type=F32)).astype(BF16)


def _fold_fno(fno_w):
    cc, sc = _dft_cos_sin(GB_DIM)
    mat = pl.BlockSpec((GB_DIM, GB_DIM), lambda g: (0, 0))
    return pl.pallas_call(
        _fold_kernel,
        grid=(G_B,),
        in_specs=[mat, mat, pl.BlockSpec((None, GB_DIM, GB_DIM), lambda g: (g, 0, 0))],
        out_specs=pl.BlockSpec((None, GB_DIM, 2 * GB_DIM), lambda g: (g, 0, 0)),
        out_shape=jax.ShapeDtypeStruct((G_B, GB_DIM, 2 * GB_DIM), BF16),
        compiler_params=_cparams(("arbitrary",)),
        name="fold_fno",
    )(jnp.asarray(cc, F32), jnp.asarray(sc, F32), fno_w)


CH_TM = 2048


def _chan_kernel(u_ref, m_ref, br_ref, bi_ref):
    r = jnp.dot(u_ref[...], m_ref[...], preferred_element_type=F32)
    br_ref[...] = r[:, :GB_DIM].astype(BF16)
    bi_ref[...] = r[:, GB_DIM:].astype(BF16)


def _chan_dft(p, m12):
    bx, lx, _ = p.shape
    tm = min(CH_TM, lx)
    col = pl.BlockSpec((None, tm, GB_DIM), lambda b, i, g: (b, i, g))
    shp = jax.ShapeDtypeStruct((bx, lx, B_WIDTH), BF16)
    return pl.pallas_call(
        _chan_kernel,
        grid=(bx, lx // tm, G_B),
        in_specs=[pl.BlockSpec((None, tm, GB_DIM), lambda b, i, g: (b, i, P_U0 // GB_DIM + g)),
                  pl.BlockSpec((None, GB_DIM, 2 * GB_DIM), lambda b, i, g: (g, 0, 0))],
        out_specs=[col, col],
        out_shape=[shp, shp],
        compiler_params=_cparams(("arbitrary", "arbitrary", "arbitrary")),
        name="chan_dft",
    )(p, m12)


SWAP_T = 16


def _chan_slow_kernel(u_ref, m_ref, w_ref, y_ref, br_scr, bi_scr):
    r = DFT_RADIX
    x = jnp.swapaxes(u_ref[...], 0, 1).reshape(SWAP_T * r, B_WIDTH)
    for g in range(G_B):
        cols = slice(g * GB_DIM, (g + 1) * GB_DIM)
        z = jnp.dot(x[:, cols], m_ref[g], preferred_element_type=F32)
        br_scr[:, :, cols] = z[:, :GB_DIM].reshape(SWAP_T, r, GB_DIM).astype(BF16)
        bi_scr[:, :, cols] = z[:, GB_DIM:].reshape(SWAP_T, r, GB_DIM).astype(BF16)
    for bl in range(SWAP_T):
        xb = jnp.concatenate([br_scr[bl], bi_scr[bl]], axis=0)
        y = jnp.dot(w_ref[bl], xb, preferred_element_type=F32)
        br_scr[bl] = y[:r].astype(BF16)
        bi_scr[bl] = y[r:].astype(BF16)
    y_ref[0] = jnp.swapaxes(br_scr[...], 0, 1)
    y_ref[1] = jnp.swapaxes(bi_scr[...], 0, 1)


def _chan_slow(p, m12):
    b = p.shape[0]
    r = DFT_RADIX
    return pl.pallas_call(
        _chan_slow_kernel,
        grid=(b, r // SWAP_T),
        in_specs=[pl.BlockSpec((None, r, SWAP_T, B_WIDTH),
                               lambda bb, j: (bb, 0, j, P_U0 // B_WIDTH)),
                  pl.BlockSpec((G_B, GB_DIM, 2 * GB_DIM), lambda bb, j: (0, 0, 0)),
                  pl.BlockSpec((SWAP_T, 2 * r, 2 * r), lambda bb, j: (j, 0, 0))],
        out_specs=pl.BlockSpec((None, 2, r, SWAP_T, B_WIDTH), lambda bb, j: (bb, 0, 0, j, 0)),
        out_shape=jax.ShapeDtypeStruct((b, 2, r, r, B_WIDTH), BF16),
        scratch_shapes=[pltpu.VMEM((SWAP_T, r, B_WIDTH), BF16)] * 2,
        compiler_params=_cparams(("arbitrary", "arbitrary")),
        name="chan_slow",
    )(p.reshape(b, r, r, P_WIDTH), m12, jnp.asarray(_slow_matrices(), BF16))


def _slow_matrices():
    n = DFT_RADIX * DFT_RADIX
    bb = np.arange(DFT_RADIX)[:, None, None]
    k1 = np.arange(DFT_RADIX)[None, :, None]
    a = np.arange(DFT_RADIX)[None, None, :]
    ang = 2.0 * np.pi * ((DFT_RADIX * a * k1 + bb * k1) % n) / n
    c, s = np.cos(ang), np.sin(ang)
    return np.concatenate([np.concatenate([c, s], axis=2), np.concatenate([-s, c], axis=2)], axis=1)


def _fast_matrix():
    c, s = _dft_cos_sin(DFT_RADIX)
    return np.concatenate([c, s], axis=1)


def _fast_kernel(w_ref, y_ref, g_ref, o_ref, o_scr, *, scale):
    w = w_ref[...]
    for kl in range(SWAP_T):
        z = jnp.concatenate([y_ref[0, kl], y_ref[1, kl]], axis=0)
        o_scr[kl] = jnp.dot(w, z, preferred_element_type=F32).astype(BF16)
    o = jnp.swapaxes(o_scr[...], 0, 1).astype(F32)
    o_ref[...] = ((o * scale) * g_ref[...].astype(F32)).astype(BF16)


def _dft_fast(y, p):
    b = y.shape[0]
    r = DFT_RADIX
    seq = r * r
    blk = pl.BlockSpec((None, r, SWAP_T, B_WIDTH), lambda bb, j: (bb, 0, j, 0))
    out = pl.pallas_call(
        functools.partial(_fast_kernel, scale=float((seq * GB_DIM) ** -0.5)),
        grid=(b, r // SWAP_T),
        in_specs=[
            pl.BlockSpec((r, 2 * r), lambda bb, j: (0, 0)),
            pl.BlockSpec((None, 2, SWAP_T, r, B_WIDTH), lambda bb, j: (bb, 0, j, 0, 0)),
            pl.BlockSpec((None, r, SWAP_T, B_WIDTH), lambda bb, j: (bb, 0, j, P_GB0 // B_WIDTH)),
        ],
        out_specs=blk,
        out_shape=jax.ShapeDtypeStruct((b, r, r, B_WIDTH), BF16),
        scratch_shapes=[pltpu.VMEM((SWAP_T, r, B_WIDTH), BF16)],
        compiler_params=_cparams(("arbitrary", "arbitrary")),
        name="dft_fast",
    )(jnp.asarray(_fast_matrix(), BF16), y, p.reshape(b, r, r, P_WIDTH))
    return out.reshape(b, seq, B_WIDTH)


def _ctx_dft_kernel(w_ref, xr_ref, xi_ref, g_ref, o_ref, *, scale):
    x = jnp.concatenate([xr_ref[...], xi_ref[...]], axis=0)
    o = jnp.dot(w_ref[...], x, preferred_element_type=F32)
    o_ref[...] = ((o * scale) * g_ref[...].astype(F32)).astype(BF16)


def _dft_ctx(br, bi, p):
    b, seq, _ = br.shape
    c, s = _dft_cos_sin(seq)
    w = jnp.asarray(np.concatenate([c, s], axis=1), BF16)
    blk = pl.BlockSpec((None, seq, B_WIDTH), lambda bb: (bb, 0, 0))
    return pl.pallas_call(
        functools.partial(_ctx_dft_kernel, scale=float((seq * GB_DIM) ** -0.5)),
        grid=(b,),
        in_specs=[pl.BlockSpec((seq, 2 * seq), lambda bb: (0, 0)), blk, blk,
                  pl.BlockSpec((None, seq, B_WIDTH), lambda bb: (bb, 0, P_GB0 // B_WIDTH))],
        out_specs=blk,
        out_shape=jax.ShapeDtypeStruct((b, seq, B_WIDTH), BF16),
        compiler_params=_cparams(("arbitrary",)),
        name="dft_ctx",
    )(w, br, bi, p)


OD_TM = 1024
OD_TN = 512
OD_RC = 256
OD_AHEAD = 4


def _inproj_odd_kernel(x_ref, sh_ref, sc_ref, nw_ref, wb_ref, wc_ref, wx_ref, wg_ref, cw_ref,
                       y_ref, h_scr, *, period, n_i):
    n = pl.program_id(2)
    tile = pl.program_id(0) * n_i + pl.program_id(1)
    slot = tile % 2
    tm = x_ref.shape[0]
    ahead_rows = tm // ((C_WIDTH // OD_TN - OD_AHEAD) * (tm // OD_RC))

    def norm_rows(rows):
        return _norm_mod(x_ref[rows, :], nw_ref[...], sh_ref[...], sc_ref[...]).astype(BF16)

    @pl.when(jnp.logical_and(tile == 0, n == 0))
    def _():
        h_scr[0] = norm_rows(slice(None))

    def compute(norm_ahead):
        cw = cw_ref[...]
        pos = lax.broadcasted_iota(jnp.int32, (OD_RC, 1), 0) % period
        for r in range(tm // OD_RC):
            rows = slice(r * OD_RC, (r + 1) * OD_RC)
            h = h_scr[slot, rows, :]
            z = (jnp.dot(h, wc_ref[...], preferred_element_type=F32)
                 * jnp.dot(h, wx_ref[...], preferred_element_type=F32))
            z_prev = jnp.where(pos == 0, 0.0, pltpu.roll(z, 1, 0))
            z_next = jnp.where(pos == period - 1, 0.0, pltpu.roll(z, OD_RC - 1, 0))
            conv = z_prev * cw[0:1, :] + z * cw[1:2, :] + z_next * cw[2:3, :]
            bg = jnp.dot(h, wb_ref[...], preferred_element_type=F32)
            g = jnp.dot(h, wg_ref[...], preferred_element_type=F32)
            y_ref[rows, :] = (bg * conv * _silu(g)).astype(BF16)
            if norm_ahead:
                piece = (n - OD_AHEAD) * (tm // OD_RC) + r
                nrows = pl.ds(pl.multiple_of(piece * ahead_rows, ahead_rows), ahead_rows)
                h_scr[1 - slot, nrows, :] = norm_rows(nrows)

    @pl.when(n < OD_AHEAD)
    def _():
        compute(False)

    @pl.when(n >= OD_AHEAD)
    def _():
        compute(True)


def _inproj_odd(x, shift, scale, nw, w, conv_w, layer, period):
    bx, lx, d = x.shape
    tm = min(OD_TM, lx)
    nblk = C_WIDTH // OD_TN
    n_i = lx // tm
    assert OD_RC % period == 0 and tm % OD_RC == 0
    x_map, mod_map = _ahead_maps(n_i, bx * n_i, OD_AHEAD)

    def w_spec(part):
        return pl.BlockSpec((None, d, OD_TN), lambda b, i, n: (layer, 0, part * nblk + n))

    return pl.pallas_call(
        functools.partial(_inproj_odd_kernel, period=period, n_i=n_i),
        grid=(bx, n_i, nblk),
        in_specs=[
            pl.BlockSpec((None, tm, d), x_map),
            pl.BlockSpec((None, 1, d), mod_map),
            pl.BlockSpec((None, 1, d), mod_map),
            pl.BlockSpec((1, d), lambda b, i, n: (0, 0)),
            w_spec(0), w_spec(1), w_spec(2), w_spec(3),
            pl.BlockSpec((None, 3, OD_TN), lambda b, i, n: (layer, 0, n)),
        ],
        out_specs=pl.BlockSpec((None, tm, OD_TN), lambda b, i, n: (b, i, n)),
        out_shape=jax.ShapeDtypeStruct((bx, lx, C_WIDTH), BF16),
        scratch_shapes=[pltpu.VMEM((2, tm, d), BF16)],
        compiler_params=_cparams(("arbitrary", "arbitrary", "arbitrary")),
        name="inproj_odd",
    )(x, shift, scale, nw, w, w, w, w, conv_w)


OUT_TM = 512


def _outproj_kernel(*refs, nparts, final):
    y_refs = refs[:nparts]
    w_refs = refs[nparts:2 * nparts]
    x_ref, g_ref, fw_ref, o_ref = refs[2 * nparts:]
    acc = jnp.dot(y_refs[0][...], w_refs[0][...], preferred_element_type=F32)
    for y_ref, w_ref in zip(y_refs[1:], w_refs[1:]):
        acc = acc + jnp.dot(y_ref[...], w_ref[...], preferred_element_type=F32)
    o = x_ref[...] + g_ref[...] * acc
    if final:
        o = (o * lax.rsqrt(jnp.mean(o * o, axis=-1, keepdims=True) + EPS)) * fw_ref[...]
    o_ref[...] = o


def _outproj(ys, w, layer, x, gate, fw, final):
    bx, lx, d = x.shape
    nparts = len(ys)
    kp = w.shape[1] // nparts
    tm = min(OUT_TM, lx)

    def w_spec(part):
        return pl.BlockSpec((None, kp, d), lambda b, i: (layer, part, 0),
                            pipeline_mode=pl.Buffered(1))

    row = pl.BlockSpec((None, tm, d), lambda b, i: (b, i, 0))
    return pl.pallas_call(
        functools.partial(_outproj_kernel, nparts=nparts, final=final),
        grid=(bx, lx // tm),
        in_specs=[pl.BlockSpec((None, tm, kp), lambda b, i: (b, i, 0)) for _ in ys]
        + [w_spec(part) for part in range(nparts)]
        + [row, pl.BlockSpec((None, 1, d), lambda b, i: (b, 0, 0)),
           pl.BlockSpec((1, d), lambda b, i: (0, 0))],
        out_specs=row,
        out_shape=jax.ShapeDtypeStruct((bx, lx, d), F32),
        compiler_params=_cparams(("arbitrary", "arbitrary")),
        name="outproj",
    )(*ys, *([w] * nparts), x, gate, fw)


def _rope_tables(seq):
    pos = jnp.arange(seq)
    row = (pos // GRID_W).astype(F32)
    col = (pos % GRID_W).astype(F32)
    nf = DK_A // 4
    inv = ROPE_BASE ** (-jnp.arange(nf, dtype=F32) / nf)
    ang = jnp.concatenate([row[:, None] * inv[None], col[:, None] * inv[None]], axis=-1)
    cos, sin = jnp.cos(ang), jnp.sin(ang)
    return jnp.concatenate([cos, cos], axis=-1), jnp.concatenate([-sin, sin], axis=-1)


def _even_layer(x, ctx, mod_x, mod_c, nw, w_in, layer, decay_logit, fno_w, w_out, tables, fw):
    b, seq, d = x.shape
    lc = ctx.shape[1]
    shift_x, scale_x, gate_x = mod_x
    shift_c, scale_c, gate_c = mod_c
    ctx_flat = ctx.reshape(1, b * lc, d)
    cosf, sinf, ones_t, zeros_t = tables
    log_gamma = -jnp.exp(decay_logit.astype(F32))

    px = _inproj_even(x, shift_x, scale_x, nw, w_in, layer, cosf, sinf)
    pc = _inproj_even(ctx_flat, shift_c, scale_c, nw, w_in, layer, ones_t, zeros_t)
    pc = pc.reshape(b, lc, P_WIDTH)

    zeros_s = jnp.zeros((b, H_A, DK_A, DV_A), F32)
    ya_c, s_f, s_b = _ret(pc, zeros_s, zeros_s, log_gamma)
    ya_x, _, _ = _ret(px, s_f, s_b, log_gamma)

    m12 = _fold_fno(fno_w)
    yb_x = _dft_fast(_chan_slow(px, m12), px)
    brc, bic = _chan_dft(pc, m12)
    yb_c = _dft_ctx(brc, bic, pc)

    x = _outproj([ya_x, yb_x], w_out, layer, x, gate_x, fw, False)
    ctx_flat = _outproj([ya_c.reshape(1, b * lc, A_WIDTH), yb_c.reshape(1, b * lc, B_WIDTH)],
                        w_out, layer, ctx_flat, gate_c, fw, False)
    return x, ctx_flat.reshape(b, lc, d)


def _odd_layer(x, ctx, mod_x, mod_c, nw, w_in, conv_w, w_out, layer, need_ctx, fw, final):
    b, seq, d = x.shape
    lc = ctx.shape[1]
    shift_x, scale_x, gate_x = mod_x
    y = _inproj_odd(x, shift_x, scale_x, nw, w_in, conv_w, layer, GRID_W)
    x = _outproj([y], w_out, layer, x, gate_x, fw, final)
    if need_ctx:
        shift_c, scale_c, gate_c = mod_c
        ctx_flat = ctx.reshape(1, b * lc, d)
        yc = _inproj_odd(ctx_flat, shift_c, scale_c, nw, w_in, conv_w, layer, lc)
        ctx = _outproj([yc], w_out, layer, ctx_flat, gate_c, fw, False).reshape(b, lc, d)
    return x, ctx


def kernel(x, c, ctx, c_ctx, ada_w, ada_b, norm_w, ev_w_in, ret_decay_logit, fno_w, ev_w_out,
           od_w_in, conv_w, od_w_out, final_norm_w):
    b, seq, d = x.shape
    lc = ctx.shape[1]
    depth = ada_w.shape[0]
    assert seq == DFT_RADIX * DFT_RADIX and seq % GRID_W == 0 and d == D_MODEL
    assert depth % 2 == 0

    rows = 16
    s = jnp.zeros((rows, d), F32).at[:b].set(c).at[b].set(c_ctx)
    mod = _ada_mod(s, ada_w, ada_b)

    ev_w_in = ev_w_in.astype(BF16)
    ev_w_out = ev_w_out.astype(BF16)
    od_w_in = od_w_in.astype(BF16)
    od_w_out = od_w_out.astype(BF16)

    fw = final_norm_w.reshape(1, d)
    cosf, sinf = _rope_tables(seq)
    tables = (cosf, sinf, jnp.ones((b * lc, DK_A), F32), jnp.zeros((b * lc, DK_A), F32))

    for i in range(depth):
        need_ctx = i < depth - 1
        mod_x = tuple(mod[i, :b, k * d:(k + 1) * d][:, None, :] for k in range(3))
        mod_c = tuple(mod[i, b:b + 1, k * d:(k + 1) * d][:, None, :] for k in range(3))
        nw = norm_w[i].reshape(1, d)
        j = i // 2
        if i % 2 == 0:
            x, ctx = _even_layer(x, ctx, mod_x, mod_c, nw, ev_w_in, j, ret_decay_logit[j],
                                 fno_w[j], ev_w_out, tables, fw)
        else:
            x, ctx = _odd_layer(x, ctx, mod_x, mod_c, nw, od_w_in, conv_w, od_w_out, j,
                                need_ctx, fw, i == depth - 1)
    return x
```

```python
import functools

import numpy as np
import jax
import jax.numpy as jnp
from jax import lax
from jax.experimental import pallas as pl
from jax.experimental.pallas import tpu as pltpu

F32 = jnp.float32
BF16 = jnp.bfloat16

D_MODEL = 2048
DEPTH = 4
GRID_W = 64
H_A = 8
DK_A = 128
DV_A = 256
QK_WIDTH = H_A * DK_A
A_WIDTH = H_A * DV_A
G_B = 8
GB_DIM = 256
B_WIDTH = G_B * GB_DIM
EVEN_IN = 2 * QK_WIDTH + 2 * A_WIDTH + 2 * B_WIDTH
C_WIDTH = 4096
ROPE_BASE = 10000.0
EPS = 1e-6
RET_CHUNK = 256
RET_HEADS = 2
DFT_RADIX = 64

P_WIDTH = EVEN_IN
P_V0 = 2 * QK_WIDTH
P_GA0 = P_V0 + A_WIDTH
P_U0 = P_GA0 + A_WIDTH
P_GB0 = P_U0 + B_WIDTH

VMEM_LIMIT = 60 * 1024 * 1024


def _cparams(sem):
    return pltpu.CompilerParams(dimension_semantics=sem, vmem_limit_bytes=VMEM_LIMIT)


def _silu(x):
    hx = 0.5 * x
    return hx + hx * jnp.tanh(hx)


def _norm_mod(x, nw, sh, sc):
    y = x * lax.rsqrt(jnp.mean(x * x, axis=-1, keepdims=True) + EPS)
    return (y * nw) * (1.0 + sc) + sh


def _cast_specs(jobs, step_index, n_steps):
    in_specs, out_specs, out_shapes, args = [], [], [], []
    for w, layer, (rb, cb) in jobs:
        _, r, c = w.shape
        ncb = c // cb
        nb = (r // rb) * ncb
        assert r % rb == 0 and c % cb == 0 and nb <= n_steps

        def block(*ids, nb=nb, ncb=ncb):
            bid = jnp.minimum(step_index(*ids), nb - 1)
            return bid // ncb, bid % ncb

        in_specs.append(pl.BlockSpec((None, rb, cb),
                                     lambda *ids, layer=layer, block=block: (layer, *block(*ids))))
        out_specs.append(pl.BlockSpec((None, rb, cb), lambda *ids, block=block: (0, *block(*ids))))
        out_shapes.append(jax.ShapeDtypeStruct((1, r, c), BF16))
        args.append(w)
    return in_specs, out_specs, out_shapes, args


def _run_casts(src_refs, dst_refs):
    for src, dst in zip(src_refs, dst_refs):
        dst[...] = src[...].astype(BF16)


ADA_TN = 768


def _ada_kernel(s_ref, w_ref, b_ref, o_ref):
    s = _silu(s_ref[...]).astype(BF16)
    w = w_ref[...].astype(BF16)
    o_ref[...] = jnp.dot(s, w, preferred_element_type=F32) + b_ref[...]


def _ada_mod(s, ada_w, ada_b):
    depth, d, n3 = ada_w.shape
    rows = s.shape[0]
    return pl.pallas_call(
        _ada_kernel,
        grid=(depth, n3 // ADA_TN),
        in_specs=[
            pl.BlockSpec((rows, d), lambda l, n: (0, 0)),
            pl.BlockSpec((None, d, ADA_TN), lambda l, n: (l, 0, n)),
            pl.BlockSpec((None, 1, ADA_TN), lambda l, n: (l, 0, n)),
        ],
        out_specs=pl.BlockSpec((None, rows, ADA_TN), lambda l, n: (l, 0, n)),
        out_shape=jax.ShapeDtypeStruct((depth, rows, n3), F32),
        compiler_params=_cparams(("arbitrary", "arbitrary")),
        name="ada_mod",
    )(s, ada_w, ada_b.reshape(depth, 1, n3))


EV_TM = 1024
EV_TN = 2048
EV_CN = 512
EV_NT = EVEN_IN // EV_TN
EV_AHEAD = 1
EV_PLAIN_STEPS = (P_V0 // EV_TN, P_U0 // EV_TN)
EV_SILU_STEPS = (P_GA0 // EV_TN, P_GB0 // EV_TN)


def _ahead_maps(n_i, n_tiles, ahead_step):
    def tile(b, i, n):
        return jnp.minimum(b * n_i + i + jnp.where(n >= ahead_step, 1, 0), n_tiles - 1)

    def x_map(b, i, n):
        t = tile(b, i, n)
        return (t // n_i, t % n_i, 0)

    def mod_map(b, i, n):
        return (tile(b, i, n) // n_i, 0, 0)

    return x_map, mod_map


def _inproj_even_kernel(x_ref, sh_ref, sc_ref, nw_ref, w_ref, cos_ref, sin_ref,
                        p_ref, h_scr, *, kscale, n_i):
    n = pl.program_id(2)
    tile = pl.program_id(0) * n_i + pl.program_id(1)
    slot = tile % 2
    tm = x_ref.shape[0]

    def norm_rows(rows):
        return _norm_mod(x_ref[rows, :], nw_ref[...], sh_ref[...], sc_ref[...]).astype(BF16)

    @pl.when(jnp.logical_and(tile == 0, n == 0))
    def _():
        h_scr[0] = norm_rows(slice(None))

    def project(c):
        cols = slice(c * EV_CN, (c + 1) * EV_CN)
        return jnp.dot(h_scr[slot], w_ref[:, cols], preferred_element_type=F32)

    chunks = range(EV_TN // EV_CN)
    ahead_rows = tm // len(EV_PLAIN_STEPS)

    def norm_ahead():
        piece = jnp.where(n == EV_PLAIN_STEPS[0], 0, 1)
        rows = pl.ds(pl.multiple_of(piece * ahead_rows, ahead_rows), ahead_rows)
        h_scr[1 - slot, rows, :] = norm_rows(rows)

    @pl.when(n == 0)
    def _():
        cosf = cos_ref[...]
        sinf = sin_ref[...]
        for c in chunks:
            acc = project(c)
            for hh in range(EV_CN // DK_A):
                col0 = c * EV_CN + hh * DK_A
                t = acc[:, hh * DK_A:(hh + 1) * DK_A]
                o = t * cosf + pltpu.roll(t, DK_A // 2, 1) * sinf
                if col0 >= QK_WIDTH:
                    o = o * kscale
                p_ref[:, col0:col0 + DK_A] = o.astype(BF16)

    @pl.when(jnp.logical_or(n == EV_PLAIN_STEPS[0], n == EV_PLAIN_STEPS[1]))
    def _():
        for c in chunks:
            p_ref[:, c * EV_CN:(c + 1) * EV_CN] = project(c).astype(BF16)
        norm_ahead()

    @pl.when(jnp.logical_or(n == EV_SILU_STEPS[0], n == EV_SILU_STEPS[1]))
    def _():
        for c in chunks:
            p_ref[:, c * EV_CN:(c + 1) * EV_CN] = _silu(project(c)).astype(BF16)


def _inproj_even(x, shift, scale, nw, w, layer, cosf, sinf):
    bx, lx, d = x.shape
    tm = min(EV_TM, lx)
    n_i = lx // tm
    x_map, mod_map = _ahead_maps(n_i, bx * n_i, EV_AHEAD)
    return pl.pallas_call(
        functools.partial(_inproj_even_kernel, kscale=DK_A ** -0.5, n_i=n_i),
        grid=(bx, n_i, EV_NT),
        in_specs=[
            pl.BlockSpec((None, tm, d), x_map),
            pl.BlockSpec((None, 1, d), mod_map),
            pl.BlockSpec((None, 1, d), mod_map),
            pl.BlockSpec((1, d), lambda b, i, n: (0, 0)),
            pl.BlockSpec((None, d, EV_TN), lambda b, i, n: (layer, 0, n)),
            pl.BlockSpec((tm, DK_A), lambda b, i, n: (i, 0)),
            pl.BlockSpec((tm, DK_A), lambda b, i, n: (i, 0)),
        ],
        out_specs=pl.BlockSpec((None, tm, EV_TN), lambda b, i, n: (b, i, n)),
        out_shape=jax.ShapeDtypeStruct((bx, lx, P_WIDTH), BF16),
        scratch_shapes=[pltpu.VMEM((2, tm, d), BF16)],
        compiler_params=_cparams(("arbitrary", "arbitrary", "arbitrary")),
        name="inproj_even",
    )(x, shift, scale, nw, w, cosf, sinf)


def _ret_kernel(lg_ref, q_ref, k_ref, v_ref, g_ref, s0f_ref, s0b_ref,
                y_ref, sf_ref, sb_ref, st_scr, *, seq, chunk, nheads):
    nc = seq // chunk
    cf = float(chunk)

    def rows(c):
        return pl.ds(pl.multiple_of(c * chunk, chunk), chunk)

    sf_ref[...] = s0f_ref[...]
    sb_ref[...] = s0b_ref[...]

    def head_ops(hh):
        h = pl.program_id(1) * nheads + hh
        lgf = lg_ref[0, h]
        lgb = lg_ref[1, h]
        qk = slice(hh * DK_A, (hh + 1) * DK_A)
        vv = slice(hh * DV_A, (hh + 1) * DV_A)
        ii = lax.broadcasted_iota(jnp.int32, (chunk, chunk), 0)
        jj = lax.broadcasted_iota(jnp.int32, (chunk, chunk), 1)
        rel = (ii - jj).astype(F32)
        dmat = jnp.where(rel >= 0.0,
                         jnp.exp(lgf * jnp.maximum(rel, 0.0)),
                         jnp.exp(lgb * jnp.maximum(-rel, 0.0)))
        row = lax.broadcasted_iota(jnp.int32, (chunk, DK_A), 0).astype(F32)
        xi_f = jnp.exp(lgf * (row + 1.0)).astype(BF16)
        zeta_f = jnp.exp(lgf * (cf - 1.0 - row)).astype(BF16)
        xi_b = jnp.exp(lgb * (cf - row)).astype(BF16)
        zeta_b = jnp.exp(lgb * row).astype(BF16)
        dec_f = jnp.exp(jnp.full((1, DV_A), lgf * cf, F32))
        dec_b = jnp.exp(jnp.full((1, DV_A), lgb * cf, F32))

        def record_and_advance(s_ref, half, c, zeta, dec):
            s = s_ref[hh]
            st_scr[hh, c, half * DK_A:(half + 1) * DK_A, :] = s.astype(BF16)
            kz = k_ref[rows(c), qk] * zeta
            s_ref[hh] = dec * s + lax.dot_general(kz, v_ref[rows(c), vv], (((0,), (0,)), ((), ())),
                                                  preferred_element_type=F32)

        def states(t):
            record_and_advance(sf_ref, 0, t, zeta_f, dec_f)
            record_and_advance(sb_ref, 1, nc - 1 - t, zeta_b, dec_b)

        def output(c):
            qn, kn, vn = q_ref[rows(c), qk], k_ref[rows(c), qk], v_ref[rows(c), vv]
            sc = lax.dot_general(qn, kn, (((1,), (1,)), ((), ())), preferred_element_type=F32)
            o = jnp.dot((sc * dmat).astype(BF16), vn, preferred_element_type=F32)
            q2 = jnp.concatenate([qn * xi_f, qn * xi_b], axis=1)
            o = o + jnp.dot(q2, st_scr[hh, c], preferred_element_type=F32)
            o = o * lax.rsqrt(jnp.mean(o * o, axis=-1, keepdims=True) + EPS)
            y_ref[rows(c), vv] = (o * g_ref[rows(c), vv].astype(F32)).astype(BF16)

        return states, output

    heads = [head_ops(hh) for hh in range(nheads)]

    def state_sweep(t, carry):
        for states, _ in heads:
            states(t)
        return carry

    def output_sweep(c, carry):
        for _, output in heads:
            output(c)
        return carry

    lax.fori_loop(0, nc, state_sweep, 0, unroll=True)
    lax.fori_loop(0, nc, output_sweep, 0, unroll=True)


def _ret(p, s0f, s0b, log_gamma):
    b, seq, _ = p.shape
    chunk = min(RET_CHUNK, seq)
    assert seq % chunk == 0
    nh = H_A if seq == chunk else RET_HEADS
    st_spec = pl.BlockSpec((None, nh, DK_A, DV_A), lambda bb, hp: (bb, hp, 0, 0))
    st_shape = jax.ShapeDtypeStruct((b, H_A, DK_A, DV_A), F32)
    return pl.pallas_call(
        functools.partial(_ret_kernel, seq=seq, chunk=chunk, nheads=nh),
        grid=(b, H_A // nh),
        in_specs=[
            pl.BlockSpec(memory_space=pltpu.SMEM),
            pl.BlockSpec((None, seq, nh * DK_A), lambda bb, hp: (bb, 0, hp)),
            pl.BlockSpec((None, seq, nh * DK_A), lambda bb, hp: (bb, 0, H_A // nh + hp)),
            pl.BlockSpec((None, seq, nh * DV_A), lambda bb, hp: (bb, 0, P_V0 // (nh * DV_A) + hp)),
            pl.BlockSpec((None, seq, nh * DV_A), lambda bb, hp: (bb, 0, P_GA0 // (nh * DV_A) + hp)),
            st_spec, st_spec,
        ],
        out_specs=[
            pl.BlockSpec((None, seq, nh * DV_A), lambda bb, hp: (bb, 0, hp)),
            st_spec, st_spec,
        ],
        out_shape=[jax.ShapeDtypeStruct((b, seq, A_WIDTH), BF16), st_shape, st_shape],
        scratch_shapes=[pltpu.VMEM((nh, seq // chunk, 2 * DK_A, DV_A), BF16)],
        compiler_params=_cparams(("arbitrary", "arbitrary")),
        name="ret",
    )(log_gamma, p, p, p, p, s0f, s0b)


def _dft_cos_sin(n):
    kk = (np.arange(n)[:, None] * np.arange(n)[None, :]) % n
    ang = 2.0 * np.pi * kk / n
    return np.cos(ang), np.sin(ang)


def _fold_kernel(cc_ref, sc_ref, w_ref, o_ref):
    w = w_ref[...]
    o_ref[:, :GB_DIM] = jnp.dot(cc_ref[...], w, precision=lax.Precision.HIGHEST,
                                preferred_element_type=F32).astype(BF16)
    o_ref[:, GB_DIM:] = (-jnp.dot(sc_ref[...], w, precision=lax.Precision.HIGHEST,
                                  preferred_element_type=F32)).astype(BF16)


def _fold_fno(fno_w):
    cc, sc = _dft_cos_sin(GB_DIM)
    mat = pl.BlockSpec((GB_DIM, GB_DIM), lambda g: (0, 0))
    return pl.pallas_call(
        _fold_kernel,
        grid=(G_B,),
        in_specs=[mat, mat, pl.BlockSpec((None, GB_DIM, GB_DIM), lambda g: (g, 0, 0))],
        out_specs=pl.BlockSpec((None, GB_DIM, 2 * GB_DIM), lambda g: (g, 0, 0)),
        out_shape=jax.ShapeDtypeStruct((G_B, GB_DIM, 2 * GB_DIM), BF16),
        compiler_params=_cparams(("arbitrary",)),
        name="fold_fno",
    )(jnp.asarray(cc, F32), jnp.asarray(sc, F32), fno_w)


SWAP_T = 16


def _chan_slow_kernel(u_ref, m_ref, w_ref, y_ref, br_scr, bi_scr):
    r = DFT_RADIX
    x = jnp.swapaxes(u_ref[...], 0, 1).reshape(SWAP_T * r, B_WIDTH)
    for g in range(G_B):
        cols = slice(g * GB_DIM, (g + 1) * GB_DIM)
        z = jnp.dot(x[:, cols], m_ref[g], preferred_element_type=F32)
        br_scr[:, :, cols] = z[:, :GB_DIM].reshape(SWAP_T, r, GB_DIM).astype(BF16)
        bi_scr[:, :, cols] = z[:, GB_DIM:].reshape(SWAP_T, r, GB_DIM).astype(BF16)
    for bl in range(SWAP_T):
        xb = jnp.concatenate([br_scr[bl], bi_scr[bl]], axis=0)
        y = jnp.dot(w_ref[bl], xb, preferred_element_type=F32)
        y_ref[0, bl] = y[:r].astype(BF16)
        y_ref[1, bl] = y[r:].astype(BF16)


def _chan_slow(p, m12):
    b = p.shape[0]
    r = DFT_RADIX
    return pl.pallas_call(
        _chan_slow_kernel,
        grid=(b, r // SWAP_T),
        in_specs=[pl.BlockSpec((None, r, SWAP_T, B_WIDTH),
                               lambda bb, j: (bb, 0, j, P_U0 // B_WIDTH)),
                  pl.BlockSpec((G_B, GB_DIM, 2 * GB_DIM), lambda bb, j: (0, 0, 0)),
                  pl.BlockSpec((SWAP_T, 2 * r, 2 * r), lambda bb, j: (j, 0, 0))],
        out_specs=pl.BlockSpec((None, 2, SWAP_T, r, B_WIDTH), lambda bb, j: (bb, 0, j, 0, 0)),
        out_shape=jax.ShapeDtypeStruct((b, 2, r, r, B_WIDTH), BF16),
        scratch_shapes=[pltpu.VMEM((SWAP_T, r, B_WIDTH), BF16)] * 2,
        compiler_params=_cparams(("arbitrary", "arbitrary")),
        name="chan_slow",
    )(p.reshape(b, r, r, P_WIDTH), m12, jnp.asarray(_slow_matrices(), BF16))


def _slow_matrices():
    n = DFT_RADIX * DFT_RADIX
    bb = np.arange(DFT_RADIX)[:, None, None]
    k1 = np.arange(DFT_RADIX)[None, :, None]
    a = np.arange(DFT_RADIX)[None, None, :]
    ang = 2.0 * np.pi * ((DFT_RADIX * a * k1 + bb * k1) % n) / n
    c, s = np.cos(ang), np.sin(ang)
    return np.concatenate([np.concatenate([c, s], axis=2), np.concatenate([-s, c], axis=2)], axis=1)


def _fast_matrix():
    c, s = _dft_cos_sin(DFT_RADIX)
    return np.concatenate([c, s], axis=1)


def _fast_kernel(w_ref, y_ref, g_ref, o_ref, o_scr, *, scale):
    zr = jnp.swapaxes(y_ref[0], 0, 1)
    zi = jnp.swapaxes(y_ref[1], 0, 1)
    w = w_ref[...]
    for kl in range(SWAP_T):
        z = jnp.concatenate([zr[kl], zi[kl]], axis=0)
        o_scr[kl] = jnp.dot(w, z, preferred_element_type=F32).astype(BF16)
    o = jnp.swapaxes(o_scr[...], 0, 1).astype(F32)
    o_ref[...] = ((o * scale) * g_ref[...].astype(F32)).astype(BF16)


def _dft_fast(y, p):
    b = y.shape[0]
    r = DFT_RADIX
    seq = r * r
    blk = pl.BlockSpec((None, r, SWAP_T, B_WIDTH), lambda bb, j: (bb, 0, j, 0))
    out = pl.pallas_call(
        functools.partial(_fast_kernel, scale=float((seq * GB_DIM) ** -0.5)),
        grid=(b, r // SWAP_T),
        in_specs=[
            pl.BlockSpec((r, 2 * r), lambda bb, j: (0, 0)),
            pl.BlockSpec((None, 2, r, SWAP_T, B_WIDTH), lambda bb, j: (bb, 0, 0, j, 0)),
            pl.BlockSpec((None, r, SWAP_T, B_WIDTH), lambda bb, j: (bb, 0, j, P_GB0 // B_WIDTH)),
        ],
        out_specs=blk,
        out_shape=jax.ShapeDtypeStruct((b, r, r, B_WIDTH), BF16),
        scratch_shapes=[pltpu.VMEM((SWAP_T, r, B_WIDTH), BF16)],
        compiler_params=_cparams(("arbitrary", "arbitrary")),
        name="dft_fast",
    )(jnp.asarray(_fast_matrix(), BF16), y, p.reshape(b, r, r, P_WIDTH))
    return out.reshape(b, seq, B_WIDTH)


def _ctx_fourier_kernel(w_ref, u_ref, m_ref, g_ref, o_ref, *, scale):
    u = u_ref[...]
    re, im = [], []
    for g in range(G_B):
        z = jnp.dot(u[:, g * GB_DIM:(g + 1) * GB_DIM], m_ref[g], preferred_element_type=F32)
        re.append(z[:, :GB_DIM].astype(BF16))
        im.append(z[:, GB_DIM:].astype(BF16))
    x = jnp.concatenate([jnp.concatenate(re, axis=1), jnp.concatenate(im, axis=1)], axis=0)
    o = jnp.dot(w_ref[...], x, preferred_element_type=F32)
    o_ref[...] = ((o * scale) * g_ref[...].astype(F32)).astype(BF16)


def _fourier_ctx(p, m12):
    b, seq, _ = p.shape
    c, s = _dft_cos_sin(seq)
    w = jnp.asarray(np.concatenate([c, s], axis=1), BF16)

    def cols(start):
        return pl.BlockSpec((None, seq, B_WIDTH), lambda bb: (bb, 0, start // B_WIDTH))

    return pl.pallas_call(
        functools.partial(_ctx_fourier_kernel, scale=float((seq * GB_DIM) ** -0.5)),
        grid=(b,),
        in_specs=[pl.BlockSpec((seq, 2 * seq), lambda bb: (0, 0)), cols(P_U0),
                  pl.BlockSpec((G_B, GB_DIM, 2 * GB_DIM), lambda bb: (0, 0, 0)), cols(P_GB0)],
        out_specs=pl.BlockSpec((None, seq, B_WIDTH), lambda bb: (bb, 0, 0)),
        out_shape=jax.ShapeDtypeStruct((b, seq, B_WIDTH), BF16),
        compiler_params=_cparams(("arbitrary",)),
        name="fourier_ctx",
    )(w, p, m12, p)


OD_TM = 1024
OD_TN = 512
OD_RC = 256
OD_AHEAD = 4


def _inproj_odd_kernel(*refs, period, n_i, n_casts):
    x_ref, sh_ref, sc_ref, nw_ref, wb_ref, wc_ref, wx_ref, wg_ref, cw_ref = refs[:9]
    cast_src = refs[9:9 + n_casts]
    y_ref = refs[9 + n_casts]
    cast_dst = refs[10 + n_casts:10 + 2 * n_casts]
    h_scr = refs[10 + 2 * n_casts]
    n = pl.program_id(2)
    tile = pl.program_id(0) * n_i + pl.program_id(1)
    slot = tile % 2
    tm = x_ref.shape[0]
    ahead_rows = tm // ((C_WIDTH // OD_TN - OD_AHEAD) * (tm // OD_RC))

    def norm_rows(rows):
        return _norm_mod(x_ref[rows, :], nw_ref[...], sh_ref[...], sc_ref[...]).astype(BF16)

    @pl.when(jnp.logical_and(tile == 0, n == 0))
    def _():
        h_scr[0] = norm_rows(slice(None))

    def compute(norm_ahead):
        _run_casts(cast_src, cast_dst)
        cw = cw_ref[...]
        pos = lax.broadcasted_iota(jnp.int32, (OD_RC, 1), 0) % period
        for r in range(tm // OD_RC):
            rows = slice(r * OD_RC, (r + 1) * OD_RC)
            h = h_scr[slot, rows, :]
            z = (jnp.dot(h, wc_ref[...], preferred_element_type=F32)
                 * jnp.dot(h, wx_ref[...], preferred_element_type=F32))
            z_prev = jnp.where(pos == 0, 0.0, pltpu.roll(z, 1, 0))
            z_next = jnp.where(pos == period - 1, 0.0, pltpu.roll(z, OD_RC - 1, 0))
            conv = z_prev * cw[0:1, :] + z * cw[1:2, :] + z_next * cw[2:3, :]
            bg = jnp.dot(h, wb_ref[...], preferred_element_type=F32)
            g = jnp.dot(h, wg_ref[...], preferred_element_type=F32)
            y_ref[rows, :] = (bg * conv * _silu(g)).astype(BF16)
            if norm_ahead:
                piece = (n - OD_AHEAD) * (tm // OD_RC) + r
                nrows = pl.ds(pl.multiple_of(piece * ahead_rows, ahead_rows), ahead_rows)
                h_scr[1 - slot, nrows, :] = norm_rows(nrows)

    @pl.when(n < OD_AHEAD)
    def _():
        compute(False)

    @pl.when(n >= OD_AHEAD)
    def _():
        compute(True)


def _inproj_odd(x, shift, scale, nw, w, conv_w, layer, period, casts=()):
    bx, lx, d = x.shape
    tm = min(OD_TM, lx)
    nblk = C_WIDTH // OD_TN
    n_i = lx // tm
    assert OD_RC % period == 0 and tm % OD_RC == 0
    x_map, mod_map = _ahead_maps(n_i, bx * n_i, OD_AHEAD)

    def w_spec(part):
        return pl.BlockSpec((None, d, OD_TN), lambda b, i, n: (0, 0, part * nblk + n))

    c_in, c_out, c_shapes, c_args = _cast_specs(
        casts, lambda b, i, n: (b * n_i + i) * nblk + n, bx * n_i * nblk)
    return pl.pallas_call(
        functools.partial(_inproj_odd_kernel, period=period, n_i=n_i, n_casts=len(casts)),
        grid=(bx, n_i, nblk),
        in_specs=[
            pl.BlockSpec((None, tm, d), x_map),
            pl.BlockSpec((None, 1, d), mod_map),
            pl.BlockSpec((None, 1, d), mod_map),
            pl.BlockSpec((1, d), lambda b, i, n: (0, 0)),
            w_spec(0), w_spec(1), w_spec(2), w_spec(3),
            pl.BlockSpec((None, 3, OD_TN), lambda b, i, n: (layer, 0, n)),
        ] + c_in,
        out_specs=[pl.BlockSpec((None, tm, OD_TN), lambda b, i, n: (b, i, n))] + c_out,
        out_shape=[jax.ShapeDtypeStruct((bx, lx, C_WIDTH), BF16)] + c_shapes,
        scratch_shapes=[pltpu.VMEM((2, tm, d), BF16)],
        compiler_params=_cparams(("arbitrary", "arbitrary", "arbitrary")),
        name="inproj_odd",
    )(x, shift, scale, nw, w, w, w, w, conv_w, *c_args)


OUT_TM = 512


def _outproj_kernel(*refs, nparts, final, n_casts):
    y_refs = refs[:nparts]
    w_refs = refs[nparts:2 * nparts]
    x_ref, g_ref, fw_ref = refs[2 * nparts:2 * nparts + 3]
    n_in = 2 * nparts + 3
    cast_src = refs[n_in:n_in + n_casts]
    o_ref = refs[n_in + n_casts]
    cast_dst = refs[n_in + n_casts + 1:]
    _run_casts(cast_src, cast_dst)
    acc = jnp.dot(y_refs[0][...], w_refs[0][...], preferred_element_type=F32)
    for y_ref, w_ref in zip(y_refs[1:], w_refs[1:]):
        acc = acc + jnp.dot(y_ref[...], w_ref[...], preferred_element_type=F32)
    o = x_ref[...] + g_ref[...] * acc
    if final:
        o = (o * lax.rsqrt(jnp.mean(o * o, axis=-1, keepdims=True) + EPS)) * fw_ref[...]
    o_ref[...] = o


def _outproj(ys, w, layer, x, gate, fw, final, casts=()):
    bx, lx, d = x.shape
    nparts = len(ys)
    kp = w.shape[1] // nparts
    tm = min(OUT_TM, lx)
    n_i = lx // tm

    def w_spec(part):
        return pl.BlockSpec((None, kp, d), lambda b, i: (layer, part, 0),
                            pipeline_mode=pl.Buffered(1))

    row = pl.BlockSpec((None, tm, d), lambda b, i: (b, i, 0))
    c_in, c_out, c_shapes, c_args = _cast_specs(casts, lambda b, i: b * n_i + i, bx * n_i)
    return pl.pallas_call(
        functools.partial(_outproj_kernel, nparts=nparts, final=final, n_casts=len(casts)),
        grid=(bx, n_i),
        in_specs=[pl.BlockSpec((None, tm, kp), lambda b, i: (b, i, 0)) for _ in ys]
        + [w_spec(part) for part in range(nparts)]
        + [row, pl.BlockSpec((None, 1, d), lambda b, i: (b, 0, 0)),
           pl.BlockSpec((1, d), lambda b, i: (0, 0))] + c_in,
        out_specs=[row] + c_out,
        out_shape=[jax.ShapeDtypeStruct((bx, lx, d), F32)] + c_shapes,
        compiler_params=_cparams(("arbitrary", "arbitrary")),
        name="outproj",
    )(*ys, *([w] * nparts), x, gate, fw, *c_args)


def _rope_tables(seq):
    pos = jnp.arange(seq)
    row = (pos // GRID_W).astype(F32)
    col = (pos % GRID_W).astype(F32)
    nf = DK_A // 4
    inv = ROPE_BASE ** (-jnp.arange(nf, dtype=F32) / nf)
    ang = jnp.concatenate([row[:, None] * inv[None], col[:, None] * inv[None]], axis=-1)
    cos, sin = jnp.cos(ang), jnp.sin(ang)
    return jnp.concatenate([cos, cos], axis=-1), jnp.concatenate([-sin, sin], axis=-1)


def _even_layer(x, ctx, mod_x, mod_c, nw, w_in, decay_logit, fno_w, w_out, tables, fw, casts):
    b, seq, d = x.shape
    lc = ctx.shape[1]
    shift_x, scale_x, gate_x = mod_x
    shift_c, scale_c, gate_c = mod_c
    ctx_flat = ctx.reshape(1, b * lc, d)
    cosf, sinf, ones_t, zeros_t = tables
    log_gamma = -jnp.exp(decay_logit.astype(F32))

    px = _inproj_even(x, shift_x, scale_x, nw, w_in, 0, cosf, sinf)
    pc = _inproj_even(ctx_flat, shift_c, scale_c, nw, w_in, 0, ones_t, zeros_t)
    pc = pc.reshape(b, lc, P_WIDTH)

    zeros_s = jnp.zeros((b, H_A, DK_A, DV_A), F32)
    ya_c, s_f, s_b = _ret(pc, zeros_s, zeros_s, log_gamma)
    ya_x, _, _ = _ret(px, s_f, s_b, log_gamma)

    m12 = _fold_fno(fno_w)
    yb_x = _dft_fast(_chan_slow(px, m12), px)
    yb_c = _fourier_ctx(pc, m12)

    x, *cast_out = _outproj([ya_x, yb_x], w_out, 0, x, gate_x, fw, False, casts)
    ctx_flat, = _outproj([ya_c.reshape(1, b * lc, A_WIDTH), yb_c.reshape(1, b * lc, B_WIDTH)],
                         w_out, 0, ctx_flat, gate_c, fw, False)
    return x, ctx_flat.reshape(b, lc, d), cast_out


def _odd_layer(x, ctx, mod_x, mod_c, nw, w_in, conv_w, conv_layer, w_out_f32, need_ctx, fw, final,
               casts):
    b, seq, d = x.shape
    lc = ctx.shape[1]
    shift_x, scale_x, gate_x = mod_x
    w_out_job = (w_out_f32, conv_layer, (512, 256))
    y, w_out, *cast_out = _inproj_odd(x, shift_x, scale_x, nw, w_in, conv_w, conv_layer, GRID_W,
                                      (w_out_job,) + tuple(casts))
    x, = _outproj([y], w_out, 0, x, gate_x, fw, final)
    if need_ctx:
        shift_c, scale_c, gate_c = mod_c
        ctx_flat = ctx.reshape(1, b * lc, d)
        yc, = _inproj_odd(ctx_flat, shift_c, scale_c, nw, w_in, conv_w, conv_layer, lc)
        ctx, = _outproj([yc], w_out, 0, ctx_flat, gate_c, fw, False)
        ctx = ctx.reshape(b, lc, d)
    return x, ctx, cast_out


def kernel(x, c, ctx, c_ctx, ada_w, ada_b, norm_w, ev_w_in, ret_decay_logit, fno_w, ev_w_out,
           od_w_in, conv_w, od_w_out, final_norm_w):
    b, seq, d = x.shape
    lc = ctx.shape[1]
    depth = ada_w.shape[0]
    assert seq == DFT_RADIX * DFT_RADIX and seq % GRID_W == 0 and d == D_MODEL
    assert depth % 2 == 0

    rows = 16
    s = jnp.zeros((rows, d), F32).at[:b].set(c).at[b].set(c_ctx)
    mod = _ada_mod(s, ada_w, ada_b)

    ev_in = ev_w_in[0:1].astype(BF16)
    ev_out = ev_w_out[0:1].astype(BF16)
    n_even = ev_w_in.shape[0]

    fw = final_norm_w.reshape(1, d)
    cosf, sinf = _rope_tables(seq)
    tables = (cosf, sinf, jnp.ones((b * lc, DK_A), F32), jnp.zeros((b * lc, DK_A), F32))

    for i in range(depth):
        need_ctx = i < depth - 1
        mod_x = tuple(mod[i, :b, k * d:(k + 1) * d][:, None, :] for k in range(3))
        mod_c = tuple(mod[i, b:b + 1, k * d:(k + 1) * d][:, None, :] for k in range(3))
        nw = norm_w[i].reshape(1, d)
        j = i // 2
        if i % 2 == 0:
            x, ctx, (od_in,) = _even_layer(
                x, ctx, mod_x, mod_c, nw, ev_in, ret_decay_logit[j], fno_w[j], ev_out, tables, fw,
                casts=((od_w_in, j, (D_MODEL, 256)),))
        else:
            nxt = j + 1
            casts = () if nxt >= n_even else ((ev_w_in, nxt, (256, 1024)), (ev_w_out, nxt, (512, 256)))
            x, ctx, nxt_w = _odd_layer(x, ctx, mod_x, mod_c, nw, od_in, conv_w, j, od_w_out,
                                       need_ctx, fw, i == depth - 1, casts)
            if nxt_w:
                ev_in, ev_out = nxt_w
    return x
```
